```python
import math
import jax, jax.numpy as jnp
from jax import lax
import numpy as np

D_MODEL = 1024
BATCH = 8
SEQ = 2048
DEPTH = 4
DEC_BATCH = 128
DEC_SEQ = 4
PAST_LEN = 2048
PAGE_SIZE = 128

N_META = 16
D_FF = 2816
H_A = 4
DK_A = 128
DV_A = 128
W_A = H_A * DV_A
CHUNK_A = 64
H_B = 4
DH_B = 64
DV_B = 2 * DH_B
W_B = H_B * DV_B
QBLOCK = 128
RMS_EPS = 1e-6
IN_SPLITS = (H_A * DK_A, H_A * DK_A, H_A * DV_A, H_A * DV_A,
             2 * H_B * DH_B, 2 * H_B * DH_B, H_B * DV_B, D_MODEL, D_MODEL)
IN_COLS = sum(IN_SPLITS)

kernel_name = 'hybrid_hgrn2_diffattn_decode_step'


def _rms(x):
    return x * lax.rsqrt(jnp.mean(x * x, axis=-1, keepdims=True) + RMS_EPS)


def rmsnorm(x, w):
    return (_rms(x.astype(jnp.float32)) * w.astype(jnp.float32)).astype(x.dtype)


def swiglu(h, wg, wu, wd):
    return (jax.nn.silu(h @ wg) * (h @ wu)) @ wd


def half_ffn(h, pre, post, wg, wu, wd):
    return h + 0.5 * rmsnorm(swiglu(rmsnorm(h, pre), wg, wu, wd), post)


def split_projection(h, w_in):
    z = h @ w_in
    points, acc = [], 0
    for s in IN_SPLITS[:-1]:
        acc += s
        points.append(acc)
    return jnp.split(z, points, axis=-1)


def mixer_front(h, w_in, lb):
    zq, zf, zi, zg, zqb, zkb, zvb, zga, zgb = split_projection(h, w_in)
    B, L = h.shape[:2]
    q_a = zq.reshape(B, L, H_A, DK_A)
    zf32 = zf.astype(jnp.float32).reshape(B, L, H_A, DK_A)
    lb = lb.reshape(H_A, DK_A)
    logf = jnp.log(lb + (1.0 - lb) * jax.nn.sigmoid(zf32))
    k_a = (1.0 - lb) * jax.nn.sigmoid(-zf32)
    v_a = zi.reshape(B, L, H_A, DV_A)
    q_b = zqb.reshape(B, L, H_B, 2 * DH_B)
    k_b = zkb.reshape(B, L, H_B, 2 * DH_B)
    v_b = zvb.reshape(B, L, H_B, DV_B)
    return q_a, k_a, v_a, logf, zg, q_b, k_b, v_b, zga, zgb


def gla_chunked(q, k, v, logf, S0, chunk):
    f32 = jnp.float32
    B, L, H, DK = q.shape
    DV = v.shape[-1]
    n = L // chunk

    def to_chunks(a):
        return a.astype(f32).reshape(B, n, chunk, H, a.shape[-1]).transpose(1, 0, 2, 3, 4)

    mask = jnp.tril(jnp.ones((chunk, chunk), dtype=bool))[None, :, :, None, None]

    def step(S, inp):
        qc, kc, vc, gc = inp
        cum = jnp.cumsum(gc, axis=1)
        o_inter = jnp.einsum('bthk,bhkv->bthv', qc * jnp.exp(cum), S)
        diff = cum[:, :, None] - cum[:, None, :]
        decay = jnp.where(mask, jnp.exp(jnp.minimum(diff, 0.0)), 0.0)
        scores = jnp.einsum('bthk,bshk,btshk->bhts', qc, kc, decay)
        o = o_inter + jnp.einsum('bhts,bshv->bthv', scores, vc)
        last = cum[:, -1]
        S = jnp.exp(last)[..., None] * S + jnp.einsum(
            'bshk,bshv->bhkv', kc * jnp.exp(last[:, None] - cum), vc)
        return S, o

    S, o = lax.scan(step, S0.astype(f32), (to_chunks(q), to_chunks(k), to_chunks(v), to_chunks(logf)))
    o = o.transpose(1, 0, 2, 3, 4).reshape(B, L, H, DV)
    return o, S


def diff_attend(q, k, v, q_pos, k_pos, lam, lam_init):
    B, Tq = q.shape[:2]
    Tk = k.shape[1]
    qc = q.reshape(B, Tq, H_B, 2, DH_B)
    kc = k.reshape(B, Tk, H_B, 2, DH_B)
    s = jnp.einsum('bqhcd,bkhcd->bhcqk', qc, kc, preferred_element_type=jnp.float32) * (DH_B ** -0.5)
    mask = k_pos[None, :] <= q_pos[:, None]
    s = jnp.where(mask, s, -jnp.inf)
    p = jax.nn.softmax(s, axis=-1)
    a = p[:, :, 0] - lam * p[:, :, 1]
    o = jnp.einsum('bhqk,bkhv->bqhv', a, v.astype(jnp.float32))
    return _rms(o) * (1.0 - lam_init)


def prompt_diff_attention(q, k, v, lam, lam_init):
    B, L = q.shape[:2]
    S = L - N_META
    pos = jnp.arange(L)
    o_meta = diff_attend(q[:, :N_META], k[:, :N_META], v[:, :N_META],
                         pos[:N_META], pos[:N_META], lam, lam_init)
    n_blk = S // QBLOCK

    def block(j):
        start = N_META + j * QBLOCK
        qs = lax.dynamic_slice_in_dim(q, start, QBLOCK, axis=1)
        return diff_attend(qs, k, v, start + jnp.arange(QBLOCK), pos, lam, lam_init)

    o_real = lax.map(block, jnp.arange(n_blk))
    o_real = o_real.transpose(1, 0, 2, 3, 4).reshape(B, S, H_B, DV_B)
    return jnp.concatenate([o_meta, o_real], axis=1)


def mixer_back(o_a, zg, o_b, zga, zgb, norm_w, w_br_a, w_br_b, w_out, dtype):
    B, L = o_a.shape[:2]
    g = jax.nn.silu(zg.astype(jnp.float32).reshape(B, L, H_A, DV_A))
    ya = (_rms(o_a) * norm_w.astype(jnp.float32) * g).reshape(B, L, W_A).astype(dtype)
    yb = o_b.reshape(B, L, W_B).astype(dtype)
    merged = jax.nn.sigmoid(zga) * (ya @ w_br_a) + jax.nn.sigmoid(zgb) * (yb @ w_br_b)
    return merged @ w_out


def setup_inputs(seed: int = 0) -> dict:
    f32 = jnp.float32
    key = jax.random.key(seed)
    ks = jax.random.split(key, 32)
    n_pages = PAST_LEN // PAGE_SIZE
    n_phys = (DEC_BATCH * n_pages * 5) // 4

    def nrm(k, shape, scale):
        return jax.random.normal(k, shape, f32) * scale

    def gain(k):
        return 1.0 + 0.05 * jax.random.normal(k, (DEPTH, D_MODEL), f32)

    perm = jax.random.permutation(ks[5], n_phys)[: DEC_BATCH * n_pages]
    page_table = perm.reshape(DEC_BATCH, n_pages).astype(jnp.int32)
    return {
        'x_prompt': nrm(ks[0], (BATCH, SEQ, D_MODEL), 1.0),
        'x_sample': nrm(ks[1], (DEC_BATCH, DEC_SEQ, D_MODEL), 1.0),
        'cache_k': nrm(ks[2], (DEPTH, n_phys, PAGE_SIZE, H_B, 2 * DH_B), 1.0),
        'cache_v': nrm(ks[3], (DEPTH, n_phys, PAGE_SIZE, H_B, DV_B), 1.0),
        'state_hgrn': nrm(ks[4], (DEPTH, DEC_BATCH, H_A, DK_A, DV_A), 0.3),
        'page_table': page_table,
        'meta_tokens': nrm(ks[6], (N_META, D_MODEL), 1.0),
        'ffn1_pre_norm': gain(ks[7]),
        'ffn1_post_norm': gain(ks[8]),
        'ffn1_w_gate': nrm(ks[9], (DEPTH, D_MODEL, D_FF), D_MODEL ** -0.5),
        'ffn1_w_up': nrm(ks[10], (DEPTH, D_MODEL, D_FF), D_MODEL ** -0.5),
        'ffn1_w_down': nrm(ks[11], (DEPTH, D_FF, D_MODEL), D_FF ** -0.5),
        'mix_pre_norm': gain(ks[12]),
        'mix_post_norm': gain(ks[13]),
        'w_in': nrm(ks[14], (DEPTH, D_MODEL, IN_COLS), D_MODEL ** -0.5),
        'hgrn_lb_logits': nrm(ks[15], (DEPTH, H_A * DK_A), 0.5),
        'hgrn_norm_w': 1.0 + 0.05 * jax.random.normal(ks[16], (DEPTH, DV_A), f32),
        'lambda_q1': nrm(ks[17], (DEPTH, DH_B), 0.1),
        'lambda_k1': nrm(ks[18], (DEPTH, DH_B), 0.1),
        'lambda_q2': nrm(ks[19], (DEPTH, DH_B), 0.1),
        'lambda_k2': nrm(ks[20], (DEPTH, DH_B), 0.1),
        'w_branch_a': nrm(ks[21], (DEPTH, W_A, D_MODEL), W_A ** -0.5),
        'w_branch_b': nrm(ks[22], (DEPTH, W_B, D_MODEL), W_B ** -0.5),
        'w_out': nrm(ks[23], (DEPTH, D_MODEL, D_MODEL), D_MODEL ** -0.5),
        'ffn2_pre_norm': gain(ks[24]),
        'ffn2_post_norm': gain(ks[25]),
        'ffn2_w_gate': nrm(ks[26], (DEPTH, D_MODEL, D_FF), D_MODEL ** -0.5),
        'ffn2_w_up': nrm(ks[27], (DEPTH, D_MODEL, D_FF), D_MODEL ** -0.5),
        'ffn2_w_down': nrm(ks[28], (DEPTH, D_FF, D_MODEL), D_FF ** -0.5),
    }


def reference(x_prompt, x_sample, cache_k, cache_v, state_hgrn, page_table,
              meta_tokens, ffn1_pre_norm, ffn1_post_norm, ffn1_w_gate, ffn1_w_up, ffn1_w_down,
              mix_pre_norm, mix_post_norm, w_in, hgrn_lb_logits, hgrn_norm_w,
              lambda_q1, lambda_k1, lambda_q2, lambda_k2, w_branch_a, w_branch_b, w_out,
              ffn2_pre_norm, ffn2_post_norm, ffn2_w_gate, ffn2_w_up, ffn2_w_down):
    f32 = jnp.float32
    B = x_prompt.shape[0]
    DB, T = x_sample.shape[:2]
    past = page_table.shape[1] * PAGE_SIZE
    lbs = jnp.cumsum(jax.nn.softmax(hgrn_lb_logits.astype(f32), axis=0), axis=0)
    lbs = lbs - lbs[0:1]
    hp = jnp.concatenate([jnp.broadcast_to(meta_tokens[None].astype(x_prompt.dtype), (B, N_META, D_MODEL)),
                          x_prompt], axis=1)
    hs = x_sample
    q_pos_s = past + jnp.arange(T)
    k_pos_s = jnp.arange(past + T)
    k_p, v_p, st_p, k_s, v_s, st_s = [], [], [], [], [], []
    for l in range(DEPTH):
        lam_init = 0.8 - 0.6 * math.exp(-0.3 * l)
        lam = (jnp.exp(jnp.sum(lambda_q1[l].astype(f32) * lambda_k1[l].astype(f32)))
               - jnp.exp(jnp.sum(lambda_q2[l].astype(f32) * lambda_k2[l].astype(f32))) + lam_init)
        hp = half_ffn(hp, ffn1_pre_norm[l], ffn1_post_norm[l], ffn1_w_gate[l], ffn1_w_up[l], ffn1_w_down[l])
        hs = half_ffn(hs, ffn1_pre_norm[l], ffn1_post_norm[l], ffn1_w_gate[l], ffn1_w_up[l], ffn1_w_down[l])
        qa, ka, va, lf, zg, qb, kb, vb, zga, zgb = mixer_front(rmsnorm(hp, mix_pre_norm[l]), w_in[l], lbs[l])
        S0 = jnp.zeros((B, H_A, DK_A, DV_A), f32)
        o_m, S_m = gla_chunked(qa[:, :N_META], ka[:, :N_META], va[:, :N_META], lf[:, :N_META], S0, N_META)
        o_r, S_p = gla_chunked(qa[:, N_META:], ka[:, N_META:], va[:, N_META:], lf[:, N_META:], S_m, CHUNK_A)
        oa = jnp.concatenate([o_m, o_r], axis=1)
        ob = prompt_diff_attention(qb, kb, vb, lam, lam_init)
        hp = hp + rmsnorm(mixer_back(oa, zg, ob, zga, zgb, hgrn_norm_w[l], w_branch_a[l], w_branch_b[l],
                                     w_out[l], hp.dtype), mix_post_norm[l])
        k_p.append(kb)
        v_p.append(vb)
        st_p.append(S_p.astype(state_hgrn.dtype))
        qa, ka, va, lf, zg, qb, kb, vb, zga, zgb = mixer_front(rmsnorm(hs, mix_pre_norm[l]), w_in[l], lbs[l])
        oa, S_s = gla_chunked(qa, ka, va, lf, state_hgrn[l], T)
        past_k = cache_k[l][page_table].reshape(DB, past, H_B, 2 * DH_B)
        past_v = cache_v[l][page_table].reshape(DB, past, H_B, DV_B)
        k_all = jnp.concatenate([past_k, kb.astype(past_k.dtype)], axis=1)
        v_all = jnp.concatenate([past_v, vb.astype(past_v.dtype)], axis=1)
        ob = diff_attend(qb, k_all, v_all, q_pos_s, k_pos_s, lam, lam_init)
        hs = hs + rmsnorm(mixer_back(oa, zg, ob, zga, zgb, hgrn_norm_w[l], w_branch_a[l], w_branch_b[l],
                                     w_out[l], hs.dtype), mix_post_norm[l])
        k_s.append(kb)
        v_s.append(vb)
        st_s.append(S_s.astype(state_hgrn.dtype))
        hp = half_ffn(hp, ffn2_pre_norm[l], ffn2_post_norm[l], ffn2_w_gate[l], ffn2_w_up[l], ffn2_w_down[l])
        hs = half_ffn(hs, ffn2_pre_norm[l], ffn2_post_norm[l], ffn2_w_gate[l], ffn2_w_up[l], ffn2_w_down[l])
    return (hp[:, N_META:], hs, jnp.stack(k_p), jnp.stack(v_p), jnp.stack(st_p),
            jnp.stack(k_s), jnp.stack(v_s), jnp.stack(st_s))
```

```python
import functools
import math

import jax
import jax.numpy as jnp
from jax import lax
from jax.experimental import pallas as pl
from jax.experimental.pallas import tpu as pltpu

F32 = jnp.float32
MXU_DTYPE = jnp.bfloat16
RMS_EPS = 1e-6
N_META = 16
N_HEADS = 4
HEAD_DIM = 128
HALF_DIM = HEAD_DIM // 2
WIDTH = N_HEADS * HEAD_DIM
PAGE = 128
GLA_CHUNK = 32
GLA_SAFE_DECAY = 80.0
ATT_TILE = 256
ATT_TAIL = 128
ROWS_PER_STEP = 16
FF_CHUNK = 1024
ROW_TILE_TARGET = 896
NEG_BIG = -1e30
VMEM_LIMIT = 56 * 1024 * 1024


def _rms(x):
    return x * lax.rsqrt(jnp.mean(x * x, axis=-1, keepdims=True) + RMS_EPS)


def _dot(a, b):
    return jnp.dot(a, b, preferred_element_type=F32)


def _dot_nt(a, b):
    return lax.dot_general(a, b, (((1,), (1,)), ((), ())), preferred_element_type=F32)


def _dot_tn(a, b):
    return lax.dot_general(a, b, (((0,), (0,)), ((), ())), preferred_element_type=F32)


def _silu(x):
    return x * jax.nn.sigmoid(x)


def _row_tile(n, target, align=16):
    best = None
    for t in range(align, min(n, target) + 1, align):
        if n % t == 0:
            best = t
    assert best is not None, (n, target)
    return best


def _chunks(total, size):
    return tuple((c, min(c + size, total)) for c in range(0, total, size))


def _params(*sem):
    return pltpu.CompilerParams(dimension_semantics=sem, vmem_limit_bytes=VMEM_LIMIT)


def _const_spec(shape):
    return pl.BlockSpec(shape, lambda *_: (0,) * len(shape), pipeline_mode=pl.Buffered(1))


def _ffn_kernel(h_ref, pre_ref, post_ref, wg_ref, wu_ref, wd_ref, o_ref, *, chunks):
    x = h_ref[...]
    xn = (_rms(x) * pre_ref[...]).astype(MXU_DTYPE)
    y = None
    for c0, c1 in chunks:
        g = _dot(xn, wg_ref[:, c0:c1])
        u = _dot(xn, wu_ref[:, c0:c1])
        part = _dot((_silu(g) * u).astype(MXU_DTYPE), wd_ref[c0:c1, :])
        y = part if y is None else y + part
    o_ref[...] = x + 0.5 * (_rms(y) * post_ref[...])


def _ffn_half(h, pre, post, wg, wu, wd, tm):
    n, d = h.shape
    f = wg.shape[1]
    row = pl.BlockSpec((tm, d), lambda i: (i, 0))
    return pl.pallas_call(
        functools.partial(_ffn_kernel, chunks=_chunks(f, FF_CHUNK)),
        grid=(n // tm,),
        in_specs=[row, _const_spec((1, d)), _const_spec((1, d)),
                  _const_spec((d, f)), _const_spec((d, f)), _const_spec((f, d))],
        out_specs=row,
        out_shape=jax.ShapeDtypeStruct((n, d), F32),
        compiler_params=_params("parallel"),
        name="ffn_half",
    )(h, pre, post, wg, wu, wd)


def _front_kernel(h_ref, pre_ref, w_ref, lb_ref,
                  qa_ref, ka_ref, va_ref, lf_ref, zg_ref, qb_ref, kb_ref, vb_ref):
    x = h_ref[...]
    xn = (_rms(x) * pre_ref[...]).astype(MXU_DTYPE)

    def proj(i):
        return _dot(xn, w_ref[:, i * WIDTH:(i + 1) * WIDTH])

    qa_ref[...] = proj(0).astype(qa_ref.dtype)
    zf = proj(1)
    lb = lb_ref[...]
    e = jnp.exp(-jnp.abs(zf))
    r = 1.0 / (1.0 + e)
    sig_pos = jnp.where(zf >= 0, r, e * r)
    sig_neg = jnp.where(zf >= 0, e * r, r)
    lf_ref[...] = jnp.log(lb + (1.0 - lb) * sig_pos)
    ka_ref[...] = ((1.0 - lb) * sig_neg).astype(ka_ref.dtype)
    va_ref[...] = proj(2).astype(va_ref.dtype)
    zg_ref[...] = proj(3).astype(zg_ref.dtype)
    qb_ref[...] = proj(4).astype(qb_ref.dtype)
    kb_ref[...] = proj(5)
    vb_ref[...] = proj(6)


def _mixer_front(h, pre, w_front, lb, tm):
    n, d = h.shape
    row = pl.BlockSpec((tm, d), lambda i: (i, 0))
    col = pl.BlockSpec((tm, WIDTH), lambda i: (i, 0))
    half = jax.ShapeDtypeStruct((n, WIDTH), MXU_DTYPE)
    full = jax.ShapeDtypeStruct((n, WIDTH), F32)
    return pl.pallas_call(
        _front_kernel,
        grid=(n // tm,),
        in_specs=[row, _const_spec((1, d)), _const_spec((d, 7 * WIDTH)),
                  _const_spec((1, WIDTH))],
        out_specs=[col] * 8,
        out_shape=[half, half, half, full, half, half, full, full],
        compiler_params=_params("parallel"),
        name="mixer_front",
    )(h, pre, w_front, lb)


def _back_kernel(h_ref, oa_ref, zg_ref, ob_ref, pre_ref, post_ref, nw_ref,
                 wgate_ref, wa_ref, wb_ref, wo_ref, o_ref):
    x = h_ref[...]
    d = x.shape[1]
    xn = (_rms(x) * pre_ref[...]).astype(MXU_DTYPE)
    oa = oa_ref[...].astype(F32)
    heads = [_rms(oa[:, h * HEAD_DIM:(h + 1) * HEAD_DIM]) * nw_ref[...]
             for h in range(N_HEADS)]
    ya = (jnp.concatenate(heads, axis=1) * _silu(zg_ref[...].astype(F32))).astype(MXU_DTYPE)
    gate_a = jax.nn.sigmoid(_dot(xn, wgate_ref[:, :d]))
    gate_b = jax.nn.sigmoid(_dot(xn, wgate_ref[:, d:]))
    merged = gate_a * _dot(ya, wa_ref[...]) + gate_b * _dot(ob_ref[...], wb_ref[...])
    out = _dot(merged.astype(MXU_DTYPE), wo_ref[...])
    o_ref[...] = x + _rms(out) * post_ref[...]


def _mixer_back(h, oa, zg, ob, pre, post, nw, w_gate, wa, wb, wo, tm):
    n, d = h.shape
    row = pl.BlockSpec((tm, d), lambda i: (i, 0))
    col = pl.BlockSpec((tm, WIDTH), lambda i: (i, 0))
    return pl.pallas_call(
        _back_kernel,
        grid=(n // tm,),
        in_specs=[row, col, col, col, _const_spec((1, d)), _const_spec((1, d)),
                  _const_spec((1, HEAD_DIM)), _const_spec((d, 2 * d)),
                  _const_spec((WIDTH, d)), _const_spec((WIDTH, d)), _const_spec((d, d))],
        out_specs=row,
        out_shape=jax.ShapeDtypeStruct((n, d), F32),
        compiler_params=_params("parallel"),
        name="mixer_back",
    )(h, oa, zg, ob, pre, post, nw, w_gate, wa, wb, wo)


def _cumsum_rows(x):
    rows = x.shape[0]
    row = lax.broadcasted_iota(jnp.int32, x.shape, 0)
    shift = 1
    while shift < rows:
        x = x + jnp.where(row >= shift, pltpu.roll(x, shift, 0), 0.0)
        shift *= 2
    return x


def _scores_direct(q, k, cum):
    c = q.shape[0]
    col = lax.broadcasted_iota(jnp.int32, (c, c), 1)
    row = lax.broadcasted_iota(jnp.int32, (c, 1), 0)

    def body(s, acc):
        pick = row == s
        ks = jnp.sum(jnp.where(pick, k, 0.0), axis=0, keepdims=True)
        cs = jnp.sum(jnp.where(pick, cum, 0.0), axis=0, keepdims=True)
        t = q * ks * jnp.exp(jnp.minimum(cum - cs, 0.0))
        return jnp.where(col == s, jnp.sum(t, axis=1, keepdims=True), acc)

    return lax.fori_loop(0, c, body, jnp.zeros((c, c), F32))


def _gla_chunk(q, k, v, g, state):
    c = q.shape[0]
    cum = _cumsum_rows(g)
    last = cum[c - 1:c, :]
    q_dec = q * jnp.exp(cum)
    k_end = k * jnp.exp(last - cum)
    state_decay = jnp.exp(last)
    safe = jnp.max(-last) <= GLA_SAFE_DECAY
    tril = (lax.broadcasted_iota(jnp.int32, (c, c), 1)
            <= lax.broadcasted_iota(jnp.int32, (c, c), 0))
    outs, new_state = [], []
    for h in range(N_HEADS):
        sl = slice(h * HEAD_DIM, (h + 1) * HEAD_DIM)
        qd = q_dec[:, sl].astype(MXU_DTYPE)
        scores = lax.cond(
            safe,
            lambda: _dot_nt(qd, (k[:, sl] * jnp.exp(-cum[:, sl])).astype(MXU_DTYPE)),
            lambda: _scores_direct(q[:, sl], k[:, sl], cum[:, sl]))
        scores = jnp.where(tril, scores, 0.0).astype(MXU_DTYPE)
        vh = v[:, sl].astype(MXU_DTYPE)
        outs.append(_dot_nt(qd, state[h].astype(MXU_DTYPE)) + _dot(scores, vh))
        new_state.append(state[h] * state_decay[:, sl]
                         + _dot_tn(vh, k_end[:, sl].astype(MXU_DTYPE)))
    return jnp.concatenate(outs, axis=1), new_state


def _gla_prompt_kernel(q_ref, k_ref, v_ref, g_ref, o_ref, s_ref):
    rows = q_ref.shape[0]

    def run(r0, c, state):
        sl = pl.ds(r0, c)
        o, state = _gla_chunk(q_ref[sl, :].astype(F32), k_ref[sl, :].astype(F32),
                              v_ref[sl, :].astype(F32), g_ref[sl, :], state)
        o_ref[sl, :] = o.astype(o_ref.dtype)
        return state

    state = [jnp.zeros((HEAD_DIM, HEAD_DIM), F32)] * N_HEADS
    state = run(0, N_META, state)

    def body(i, state):
        return tuple(run(pl.multiple_of(N_META + i * GLA_CHUNK, 16), GLA_CHUNK, list(state)))

    state = lax.fori_loop(0, (rows - N_META) // GLA_CHUNK, body, tuple(state))
    for h in range(N_HEADS):
        s_ref[0, h] = state[h].T


def _gla_prompt(qa, ka, va, lf, batch, seq_rows):
    blk = pl.BlockSpec((seq_rows, WIDTH), lambda b: (b, 0))
    return pl.pallas_call(
        _gla_prompt_kernel,
        grid=(batch,),
        in_specs=[blk] * 4,
        out_specs=[blk, pl.BlockSpec((1, N_HEADS, HEAD_DIM, HEAD_DIM), lambda b: (b, 0, 0, 0))],
        out_shape=[jax.ShapeDtypeStruct((batch * seq_rows, WIDTH), MXU_DTYPE),
                   jax.ShapeDtypeStruct((batch, N_HEADS, HEAD_DIM, HEAD_DIM), F32)],
        compiler_params=_params("parallel"),
        name="gla_prompt",
    )(qa, ka, va, lf)


def _gla_sample_kernel(q_ref, k_ref, v_ref, g_ref, s0_ref, o_ref, s_ref, *, steps):
    q = q_ref[...].astype(F32)
    k = k_ref[...].astype(F32)
    v = v_ref[...].astype(F32)
    g = g_ref[...]
    row = lax.broadcasted_iota(jnp.int32, q.shape, 0)
    out = jnp.zeros(q.shape, F32)
    for bi in range(ROWS_PER_STEP // steps):
        mine = (row >= bi * steps) & (row < (bi + 1) * steps)
        state = [s0_ref[bi, h].T for h in range(N_HEADS)]
        o, state = _gla_chunk(q, jnp.where(mine, k, 0.0), v, jnp.where(mine, g, 0.0), state)
        out = jnp.where(mine, o, out)
        for h in range(N_HEADS):
            s_ref[bi, h] = state[h].T
    o_ref[...] = out.astype(o_ref.dtype)


def _gla_sample(qa, ka, va, lf, s0, row0, dec_batch, steps):
    per = ROWS_PER_STEP // steps
    blk = pl.BlockSpec((ROWS_PER_STEP, WIDTH), lambda i: (row0 // ROWS_PER_STEP + i, 0))
    sblk = pl.BlockSpec((per, N_HEADS, HEAD_DIM, HEAD_DIM), lambda i: (i, 0, 0, 0))
    return pl.pallas_call(
        functools.partial(_gla_sample_kernel, steps=steps),
        grid=(dec_batch // per,),
        in_specs=[blk] * 4 + [sblk],
        out_specs=[pl.BlockSpec((ROWS_PER_STEP, WIDTH), lambda i: (i, 0)), sblk],
        out_shape=[jax.ShapeDtypeStruct((dec_batch * steps, WIDTH), MXU_DTYPE),
                   jax.ShapeDtypeStruct(s0.shape, F32)],
        compiler_params=_params("parallel"),
        name="gla_sample",
    )(qa, ka, va, lf, s0)


def _diff_finish(o1, o2, lam, scale, axis):
    d = o1 - lam * o2
    return d * lax.rsqrt(jnp.mean(d * d, axis=axis, keepdims=True) + RMS_EPS) * scale


def _attn_prompt_kernel(sc_ref, q_ref, k_ref, v_ref, o_ref, kbf_ref, vt_ref):
    rows = q_ref.shape[0]
    n_full = rows // ATT_TILE
    rem = rows - n_full * ATT_TILE
    lam = sc_ref[0]
    out_scale = sc_ref[1]

    kbf_ref[0:rows, :] = k_ref[...].astype(MXU_DTYPE)
    kbf_ref[rows:, :] = jnp.zeros((kbf_ref.shape[0] - rows, HEAD_DIM), MXU_DTYPE)
    for i in range(n_full):
        vt_ref[i] = v_ref[i * ATT_TILE:(i + 1) * ATT_TILE, :].T.astype(MXU_DTYPE)
    v_tail = jnp.concatenate(
        [v_ref[n_full * ATT_TILE:rows, :], jnp.zeros((ATT_TILE - rem, HEAD_DIM), F32)], axis=0)
    vt_ref[n_full] = v_tail.T.astype(MXU_DTYPE)

    def q_tile(q0, tq, out_rows):
        q_t = q_ref[q0:q0 + tq, :].astype(F32).T * (HALF_DIM ** -0.5)
        comp = lax.broadcasted_iota(jnp.int32, q_t.shape, 0) < HALF_DIM
        q_both = jnp.concatenate([jnp.where(comp, q_t, 0.0), jnp.where(comp, 0.0, q_t)],
                                 axis=1).astype(MXU_DTYPE)

        def step(carry, k_tile, v_tile_t, kv0):
            m, l, acc = carry
            s = _dot(k_tile, q_both)
            if kv0 is not None:
                kpos = kv0 + lax.broadcasted_iota(jnp.int32, s.shape, 0)
                qcol = lax.broadcasted_iota(jnp.int32, s.shape, 1)
                qpos = q0 + jnp.where(qcol >= tq, qcol - tq, qcol)
                s = jnp.where(kpos <= qpos, s, NEG_BIG)
            m_new = jnp.maximum(m, jnp.max(s, axis=0, keepdims=True))
            alpha = jnp.exp(m - m_new)
            p = jnp.exp(s - m_new)
            l = alpha * l + jnp.sum(p, axis=0, keepdims=True)
            acc = alpha * acc + _dot(v_tile_t, p.astype(MXU_DTYPE))
            return m_new, l, acc

        carry = (jnp.full((1, 2 * tq), NEG_BIG, F32), jnp.zeros((1, 2 * tq), F32),
                 jnp.zeros((HEAD_DIM, 2 * tq), F32))
        visible = q0 // ATT_TILE

        def body(i, carry):
            r0 = pl.multiple_of(i * ATT_TILE, ATT_TILE)
            return step(carry, kbf_ref[pl.ds(r0, ATT_TILE), :], vt_ref[i], None)

        carry = lax.fori_loop(0, visible, body, carry)
        for i in range(visible, n_full):
            if i * ATT_TILE < q0 + tq:
                carry = step(carry, kbf_ref[i * ATT_TILE:(i + 1) * ATT_TILE, :], vt_ref[i],
                             i * ATT_TILE)
        if n_full * ATT_TILE < q0 + tq:
            t0 = n_full * ATT_TILE
            carry = step(carry, kbf_ref[t0:t0 + ATT_TAIL, :], vt_ref[n_full][:, :ATT_TAIL], t0)
        _, l, acc = carry
        o = acc / l
        d = _diff_finish(o[:, :tq], o[:, tq:], lam, out_scale, 0)
        o_ref[q0 + tq - out_rows:q0 + tq, :] = d.T[tq - out_rows:, :].astype(o_ref.dtype)

    for j in range(n_full):
        q_tile(j * ATT_TILE, ATT_TILE, ATT_TILE)
    if rem:
        q_tile(rows - ATT_TAIL, ATT_TAIL, rem)


def _attn_prompt(scalars, qb, kb, vb, batch, seq_rows):
    assert seq_rows >= ATT_TILE and seq_rows % ATT_TILE <= ATT_TAIL
    n_full = seq_rows // ATT_TILE
    blk = pl.BlockSpec((seq_rows, HEAD_DIM), lambda b, h: (b, h))
    return pl.pallas_call(
        _attn_prompt_kernel,
        grid=(batch, N_HEADS),
        in_specs=[pl.BlockSpec(memory_space=pltpu.SMEM), blk, blk, blk],
        out_specs=blk,
        out_shape=jax.ShapeDtypeStruct((batch * seq_rows, WIDTH), MXU_DTYPE),
        scratch_shapes=[pltpu.VMEM((n_full * ATT_TILE + ATT_TAIL, HEAD_DIM), MXU_DTYPE),
                        pltpu.VMEM((n_full + 1, HEAD_DIM, ATT_TILE), MXU_DTYPE)],
        compiler_params=_params("parallel", "parallel"),
        name="attn_prompt",
    )(scalars, qb, kb, vb)


def _attn_sample_kernel(pt_ref, sc_ref, q_ref, kn_ref, vn_ref, ck_ref, cv_ref, o_ref,
                        kbuf, vbuf, sem, *, layer, n_pages, dec_batch, steps):
    step_id = pl.program_id(0)
    per = ROWS_PER_STEP // steps
    lam = sc_ref[0]
    out_scale = sc_ref[1]

    def copies(b, slot):
        out = []
        for p in range(n_pages):
            page = pt_ref[b * n_pages + p]
            dst = pl.ds(p * PAGE, PAGE)
            out.append(pltpu.make_async_copy(ck_ref.at[layer, page], kbuf.at[slot, dst],
                                             sem.at[slot, 0]))
            out.append(pltpu.make_async_copy(cv_ref.at[layer, page], vbuf.at[slot, dst],
                                             sem.at[slot, 1]))
        return out

    @pl.when(step_id == 0)
    def _():
        for c in copies(0, 0):
            c.start()

    q_all = q_ref[...].astype(F32) * (HALF_DIM ** -0.5)
    k_new = kn_ref[...].astype(MXU_DTYPE)
    v_new = vn_ref[...].astype(MXU_DTYPE)
    n_q = 2 * N_HEADS * 8
    g_row = lax.broadcasted_iota(jnp.int32, (n_q, WIDTH), 0) // 8
    g_lane = lax.broadcasted_iota(jnp.int32, (n_q, WIDTH), 1) // HALF_DIM
    new_row = lax.broadcasted_iota(jnp.int32, (n_q, ROWS_PER_STEP), 0) % 8
    new_col = lax.broadcasted_iota(jnp.int32, (n_q, ROWS_PER_STEP), 1)
    results = []
    for bi in range(per):
        b = step_id * per + bi
        slot = bi % 2

        @pl.when(b + 1 < dec_batch)
        def _():
            for c in copies(b + 1, 1 - slot):
                c.start()

        for c in copies(b, slot):
            c.wait()

        q8 = q_all[8 * (bi * steps // 8):8 * (bi * steps // 8) + 8, :]
        q_bd = jnp.where(g_row == g_lane, jnp.concatenate([q8] * (2 * N_HEADS), axis=0),
                         0.0).astype(MXU_DTYPE)
        s_past = _dot_nt(q_bd, kbuf[slot].astype(MXU_DTYPE))
        s_new = _dot_nt(q_bd, k_new)
        new_ok = (new_col // steps == bi) & (new_col % steps <= new_row % steps)
        s_new = jnp.where(new_ok, s_new, NEG_BIG)
        m = jnp.maximum(jnp.max(s_past, axis=1, keepdims=True),
                        jnp.max(s_new, axis=1, keepdims=True))
        p_past = jnp.exp(s_past - m)
        p_new = jnp.exp(s_new - m)
        l = jnp.sum(p_past, axis=1, keepdims=True) + jnp.sum(p_new, axis=1, keepdims=True)
        o = (_dot(p_past.astype(MXU_DTYPE), vbuf[slot].astype(MXU_DTYPE))
             + _dot(p_new.astype(MXU_DTYPE), v_new)) / l
        heads = []
        for h in range(N_HEADS):
            sl = slice(h * HEAD_DIM, (h + 1) * HEAD_DIM)
            heads.append(_diff_finish(o[16 * h:16 * h + 8, sl], o[16 * h + 8:16 * h + 16, sl],
                                      lam, out_scale, 1))
        results.append(jnp.concatenate(heads, axis=1))

    row8 = lax.broadcasted_iota(jnp.int32, (8, WIDTH), 0)
    merged = []
    for pair in range(ROWS_PER_STEP // 8):
        acc = results[pair * (8 // steps)]
        for j in range(1, 8 // steps):
            acc = jnp.where(row8 // steps == j, results[pair * (8 // steps) + j], acc)
        merged.append(acc)
    o_ref[...] = jnp.concatenate(merged, axis=0).astype(o_ref.dtype)


def _attn_sample(page_table, scalars, qb, kb, vb, cache_k, cache_v, layer, row0, dec_batch, steps):
    assert 8 % steps == 0
    n_pages = page_table.shape[1]
    per = ROWS_PER_STEP // steps
    blk = pl.BlockSpec((ROWS_PER_STEP, WIDTH), lambda i, pt: (row0 // ROWS_PER_STEP + i, 0))
    grid_spec = pltpu.PrefetchScalarGridSpec(
        num_scalar_prefetch=1,
        grid=(dec_batch // per,),
        in_specs=[pl.BlockSpec(memory_space=pltpu.SMEM), blk, blk, blk,
                  pl.BlockSpec(memory_space=pl.ANY), pl.BlockSpec(memory_space=pl.ANY)],
        out_specs=pl.BlockSpec((ROWS_PER_STEP, WIDTH), lambda i, pt: (i, 0)),
        scratch_shapes=[pltpu.VMEM((2, n_pages * PAGE, WIDTH), F32),
                        pltpu.VMEM((2, n_pages * PAGE, WIDTH), F32),
                        pltpu.SemaphoreType.DMA((2, 2))],
    )
    return pl.pallas_call(
        functools.partial(_attn_sample_kernel, layer=layer, n_pages=n_pages,
                          dec_batch=dec_batch, steps=steps),
        grid_spec=grid_spec,
        out_shape=jax.ShapeDtypeStruct((dec_batch * steps, WIDTH), MXU_DTYPE),
        compiler_params=_params("arbitrary"),
        name="attn_sample",
    )(page_table.reshape(-1), scalars, qb, kb, vb, cache_k, cache_v)


def kernel(x_prompt, x_sample, cache_k, cache_v, state_hgrn, page_table, meta_tokens, ffn1_pre_norm, ffn1_post_norm, ffn1_w_gate, ffn1_w_up, ffn1_w_down, mix_pre_norm, mix_post_norm, w_in, hgrn_lb_logits, hgrn_norm_w, lambda_q1, lambda_k1, lambda_q2, lambda_k2, w_branch_a, w_branch_b, w_out, ffn2_pre_norm, ffn2_post_norm, ffn2_w_gate, ffn2_w_up, ffn2_w_down):
    batch, seq, d = x_prompt.shape
    dec_batch, steps = x_sample.shape[:2]
    depth = w_in.shape[0]
    seq_rows = N_META + seq
    n_prompt = batch * seq_rows
    n_sample = dec_batch * steps
    n = n_prompt + n_sample
    assert n_prompt % ROWS_PER_STEP == 0 and ROWS_PER_STEP % steps == 0
    assert (seq_rows - N_META) % GLA_CHUNK == 0
    tm = _row_tile(n, ROW_TILE_TARGET)

    lbs = jnp.cumsum(jax.nn.softmax(hgrn_lb_logits.astype(F32), axis=0), axis=0)
    lbs = lbs - lbs[0:1]
    lam_init = jnp.asarray([0.8 - 0.6 * math.exp(-0.3 * l) for l in range(depth)], F32)
    lam = (jnp.exp(jnp.sum(lambda_q1.astype(F32) * lambda_k1.astype(F32), axis=-1))
           - jnp.exp(jnp.sum(lambda_q2.astype(F32) * lambda_k2.astype(F32), axis=-1)) + lam_init)
    attn_scalars = jnp.stack([lam, 1.0 - lam_init], axis=1)

    meta = jnp.broadcast_to(meta_tokens[None].astype(x_prompt.dtype), (batch, N_META, d))
    h = jnp.concatenate([jnp.concatenate([meta, x_prompt], axis=1).reshape(n_prompt, d),
                         x_sample.reshape(n_sample, d)], axis=0)
    cache_k = cache_k.reshape(cache_k.shape[:3] + (WIDTH,))
    cache_v = cache_v.reshape(cache_v.shape[:3] + (WIDTH,))
    n_front = 7 * WIDTH

    def cast(w):
        return w.astype(MXU_DTYPE)

    def vec(w):
        return w.reshape(1, -1).astype(F32)

    k_p, v_p, st_p, k_s, v_s, st_s = [], [], [], [], [], []
    for l in range(depth):
        h = _ffn_half(h, vec(ffn1_pre_norm[l]), vec(ffn1_post_norm[l]), cast(ffn1_w_gate[l]),
                      cast(ffn1_w_up[l]), cast(ffn1_w_down[l]), tm)
        w_l = cast(w_in[l])
        qa, ka, va, lf, zg, qb, kb, vb = _mixer_front(
            h, vec(mix_pre_norm[l]), w_l[:, :n_front], vec(lbs[l]), tm)
        oa_p, s_p = _gla_prompt(qa, ka, va, lf, batch, seq_rows)
        ob_p = _attn_prompt(attn_scalars[l], qb, kb, vb, batch, seq_rows)
        oa_s, s_s = _gla_sample(qa, ka, va, lf, state_hgrn[l].astype(F32), n_prompt,
                                dec_batch, steps)
        ob_s = _attn_sample(page_table, attn_scalars[l], qb, kb, vb, cache_k, cache_v, l,
                            n_prompt, dec_batch, steps)
        h = _mixer_back(h, jnp.concatenate([oa_p, oa_s], axis=0), zg,
                        jnp.concatenate([ob_p, ob_s], axis=0), vec(mix_pre_norm[l]),
                        vec(mix_post_norm[l]), vec(hgrn_norm_w[l]), w_l[:, n_front:],
                        cast(w_branch_a[l]), cast(w_branch_b[l]), cast(w_out[l]), tm)
        h = _ffn_half(h, vec(ffn2_pre_norm[l]), vec(ffn2_post_norm[l]), cast(ffn2_w_gate[l]),
                      cast(ffn2_w_up[l]), cast(ffn2_w_down[l]), tm)
        k_p.append(kb[:n_prompt].reshape(batch, seq_rows, N_HEADS, HEAD_DIM))
        v_p.append(vb[:n_prompt].reshape(batch, seq_rows, N_HEADS, HEAD_DIM))
        st_p.append(s_p.astype(state_hgrn.dtype))
        k_s.append(kb[n_prompt:].reshape(dec_batch, steps, N_HEADS, HEAD_DIM))
        v_s.append(vb[n_prompt:].reshape(dec_batch, steps, N_HEADS, HEAD_DIM))
        st_s.append(s_s.astype(state_hgrn.dtype))
    y_prompt = h[:n_prompt].reshape(batch, seq_rows, d)[:, N_META:]
    y_sample = h[n_prompt:].reshape(dec_batch, steps, d)
    return (y_prompt, y_sample, jnp.stack(k_p), jnp.stack(v_p), jnp.stack(st_p),
            jnp.stack(k_s), jnp.stack(v_s), jnp.stack(st_s))
```

```python
import functools
import math

import jax
import jax.numpy as jnp
from jax import lax
from jax.experimental import pallas as pl
from jax.experimental.pallas import tpu as pltpu

F32 = jnp.float32
MXU_DTYPE = jnp.bfloat16
RMS_EPS = 1e-6
N_META = 16
N_HEADS = 4
HEAD_DIM = 128
HALF_DIM = HEAD_DIM // 2
WIDTH = N_HEADS * HEAD_DIM
PAGE = 128
GLA_CHUNK = 64
GLA_UNROLL = 2
GLA_SAFE_DECAY = 80.0
ATT_TILE = 256
ATT_TAIL = 128
ONES_ROWS = 16
LOG2_E = 1.4426950408889634
ROWS_PER_STEP = 16
FF_CHUNK = 1024
ROW_TILE_TARGET = 896
NEG_BIG = -1e30
VMEM_LIMIT = 56 * 1024 * 1024


def _rms(x):
    return x * lax.rsqrt(jnp.mean(x * x, axis=-1, keepdims=True) + RMS_EPS)


def _dot(a, b):
    return jnp.dot(a, b, preferred_element_type=F32)


def _dot_nt(a, b):
    return lax.dot_general(a, b, (((1,), (1,)), ((), ())), preferred_element_type=F32)


def _dot_tn(a, b):
    return lax.dot_general(a, b, (((0,), (0,)), ((), ())), preferred_element_type=F32)


def _silu(x):
    return x * jax.nn.sigmoid(x)


def _row_tile(n, target, align=16):
    best = None
    for t in range(align, min(n, target) + 1, align):
        if n % t == 0:
            best = t
    assert best is not None, (n, target)
    return best


def _chunks(total, size):
    return tuple((c, min(c + size, total)) for c in range(0, total, size))


def _params(*sem):
    return pltpu.CompilerParams(dimension_semantics=sem, vmem_limit_bytes=VMEM_LIMIT)


def _const_spec(shape):
    return pl.BlockSpec(shape, lambda *_: (0,) * len(shape), pipeline_mode=pl.Buffered(1))


def _ffn_kernel(h_ref, pre_ref, post_ref, wg_ref, wu_ref, wd_ref, o_ref, *, chunks):
    x = h_ref[...]
    xn = (_rms(x) * pre_ref[...]).astype(MXU_DTYPE)
    y = None
    for c0, c1 in chunks:
        g = _dot(xn, wg_ref[:, c0:c1])
        u = _dot(xn, wu_ref[:, c0:c1])
        part = _dot((_silu(g) * u).astype(MXU_DTYPE), wd_ref[c0:c1, :])
        y = part if y is None else y + part
    o_ref[...] = x + 0.5 * (_rms(y) * post_ref[...])


def _ffn_half(h, pre, post, wg, wu, wd, tm):
    n, d = h.shape
    f = wg.shape[1]
    row = pl.BlockSpec((tm, d), lambda i: (i, 0))
    return pl.pallas_call(
        functools.partial(_ffn_kernel, chunks=_chunks(f, FF_CHUNK)),
        grid=(n // tm,),
        in_specs=[row, _const_spec((1, d)), _const_spec((1, d)),
                  _const_spec((d, f)), _const_spec((d, f)), _const_spec((f, d))],
        out_specs=row,
        out_shape=jax.ShapeDtypeStruct((n, d), F32),
        compiler_params=_params("parallel"),
        name="ffn_half",
    )(h, pre, post, wg, wu, wd)


def _front_kernel(h_ref, pre_ref, w_ref, lb_ref,
                  qa_ref, ka_ref, va_ref, lf_ref, zg_ref, qb_ref, kb_ref, vb_ref,
                  kout_ref, vout_ref):
    x = h_ref[...]
    xn = (_rms(x) * pre_ref[...]).astype(MXU_DTYPE)

    def proj(i):
        return _dot(xn, w_ref[:, i * WIDTH:(i + 1) * WIDTH])

    qa_ref[...] = proj(0).astype(qa_ref.dtype)
    zf = proj(1)
    lb = lb_ref[...]
    e = jnp.exp(-jnp.abs(zf))
    r = 1.0 / (1.0 + e)
    sig_pos = jnp.where(zf >= 0, r, e * r)
    sig_neg = jnp.where(zf >= 0, e * r, r)
    lf_ref[...] = jnp.log(lb + (1.0 - lb) * sig_pos)
    ka_ref[...] = ((1.0 - lb) * sig_neg).astype(ka_ref.dtype)
    va_ref[...] = proj(2).astype(va_ref.dtype)
    zg_ref[...] = proj(3).astype(zg_ref.dtype)
    qb_ref[...] = proj(4).astype(qb_ref.dtype)
    tm = x.shape[0]
    for src, dst_ref, out_ref in ((proj(5), kb_ref, kout_ref), (proj(6), vb_ref, vout_ref)):
        dst_ref[...] = src.astype(dst_ref.dtype)
        for hd in range(N_HEADS):
            out_ref[pl.ds(hd, tm, stride=N_HEADS), :] = src[:, hd * HEAD_DIM:(hd + 1) * HEAD_DIM]


def _mixer_front(h, pre, w_front, lb, tm):
    n, d = h.shape
    row = pl.BlockSpec((tm, d), lambda i: (i, 0))
    col = pl.BlockSpec((tm, WIDTH), lambda i: (i, 0))
    half = jax.ShapeDtypeStruct((n, WIDTH), MXU_DTYPE)
    full = jax.ShapeDtypeStruct((n, WIDTH), F32)
    per_head = jax.ShapeDtypeStruct((n * N_HEADS, HEAD_DIM), F32)
    per_head_blk = pl.BlockSpec((tm * N_HEADS, HEAD_DIM), lambda i: (i, 0))
    return pl.pallas_call(
        _front_kernel,
        grid=(n // tm,),
        in_specs=[row, _const_spec((1, d)), _const_spec((d, 7 * WIDTH)),
                  _const_spec((1, WIDTH))],
        out_specs=[col] * 8 + [per_head_blk] * 2,
        out_shape=[half, half, half, full, half, half, half, half, per_head, per_head],
        compiler_params=_params("parallel"),
        name="mixer_front",
    )(h, pre, w_front, lb)


def _back_kernel(h_ref, oa_ref, zg_ref, ob_ref, pre_ref, post_ref, nw_ref,
                 wgate_ref, wa_ref, wb_ref, wo_ref, o_ref):
    x = h_ref[...]
    d = x.shape[1]
    xn = (_rms(x) * pre_ref[...]).astype(MXU_DTYPE)
    oa = oa_ref[...].astype(F32)
    heads = [_rms(oa[:, h * HEAD_DIM:(h + 1) * HEAD_DIM]) * nw_ref[...]
             for h in range(N_HEADS)]
    ya = (jnp.concatenate(heads, axis=1) * _silu(zg_ref[...].astype(F32))).astype(MXU_DTYPE)
    gate_a = jax.nn.sigmoid(_dot(xn, wgate_ref[:, :d]))
    gate_b = jax.nn.sigmoid(_dot(xn, wgate_ref[:, d:]))
    merged = gate_a * _dot(ya, wa_ref[...]) + gate_b * _dot(ob_ref[...], wb_ref[...])
    out = _dot(merged.astype(MXU_DTYPE), wo_ref[...])
    o_ref[...] = x + _rms(out) * post_ref[...]


def _mixer_back(h, oa, zg, ob, pre, post, nw, w_gate, wa, wb, wo, tm):
    n, d = h.shape
    row = pl.BlockSpec((tm, d), lambda i: (i, 0))
    col = pl.BlockSpec((tm, WIDTH), lambda i: (i, 0))
    return pl.pallas_call(
        _back_kernel,
        grid=(n // tm,),
        in_specs=[row, col, col, col, _const_spec((1, d)), _const_spec((1, d)),
                  _const_spec((1, HEAD_DIM)), _const_spec((d, 2 * d)),
                  _const_spec((WIDTH, d)), _const_spec((WIDTH, d)), _const_spec((d, d))],
        out_specs=row,
        out_shape=jax.ShapeDtypeStruct((n, d), F32),
        compiler_params=_params("parallel"),
        name="mixer_back",
    )(h, oa, zg, ob, pre, post, nw, w_gate, wa, wb, wo)


def _cumsum_rows(x):
    rows = x.shape[0]
    row = lax.broadcasted_iota(jnp.int32, x.shape, 0)
    shift = 1
    while shift < rows:
        x = x + jnp.where(row >= shift, pltpu.roll(x, shift, 0), 0.0)
        shift *= 2
    return x


def _scores_direct(q, k, cum):
    c = q.shape[0]
    col = lax.broadcasted_iota(jnp.int32, (c, c), 1)
    row = lax.broadcasted_iota(jnp.int32, (c, 1), 0)

    def body(s, acc):
        pick = row == s
        ks = jnp.sum(jnp.where(pick, k, 0.0), axis=0, keepdims=True)
        cs = jnp.sum(jnp.where(pick, cum, 0.0), axis=0, keepdims=True)
        t = q * ks * jnp.exp(jnp.minimum(cum - cs, 0.0))
        return jnp.where(col == s, jnp.sum(t, axis=1, keepdims=True), acc)

    return lax.fori_loop(0, c, body, jnp.zeros((c, c), F32))


def _gla_chunk(q, k, v, g, state, direct):
    c = q.shape[0]
    cum = _cumsum_rows(g)
    last = cum[c - 1:c, :]
    q_dec = (q * jnp.exp(cum)).astype(MXU_DTYPE)
    k_end = (k * jnp.exp(last - cum)).astype(MXU_DTYPE)
    v_mx = v.astype(MXU_DTYPE)
    state_decay = jnp.exp(last)
    if not direct:
        k_inv = (k * jnp.exp(-cum)).astype(MXU_DTYPE)
    tril = (lax.broadcasted_iota(jnp.int32, (c, c), 1)
            <= lax.broadcasted_iota(jnp.int32, (c, c), 0))
    outs, new_state = [], []
    for h in range(N_HEADS):
        sl = slice(h * HEAD_DIM, (h + 1) * HEAD_DIM)
        if direct:
            scores = _scores_direct(q[:, sl], k[:, sl], cum[:, sl])
        else:
            scores = _dot_nt(q_dec[:, sl], k_inv[:, sl])
        scores = jnp.where(tril, scores, 0.0).astype(MXU_DTYPE)
        outs.append(_dot_nt(q_dec[:, sl], state[h].astype(MXU_DTYPE)) + _dot(scores, v_mx[:, sl]))
        new_state.append(state[h] * state_decay[:, sl] + _dot_tn(v_mx[:, sl], k_end[:, sl]))
    return jnp.concatenate(outs, axis=1), new_state


def _gla_prompt_kernel(q_ref, k_ref, v_ref, g_ref, o_ref, s_ref):
    rows = q_ref.shape[0]
    n_chunks = (rows - N_META) // GLA_CHUNK

    def run(r0, c, state, direct):
        sl = pl.ds(r0, c)
        o, state = _gla_chunk(q_ref[sl, :].astype(F32), k_ref[sl, :].astype(F32),
                              v_ref[sl, :].astype(F32), g_ref[sl, :], state, direct)
        o_ref[sl, :] = o.astype(o_ref.dtype)
        return state

    def run_all(direct):
        state = [jnp.zeros((HEAD_DIM, HEAD_DIM), F32)] * N_HEADS
        state = run(0, N_META, state, direct)

        def body(i, state):
            state = list(state)
            for u in range(GLA_UNROLL):
                r0 = pl.multiple_of(N_META + (i * GLA_UNROLL + u) * GLA_CHUNK, 16)
                state = run(r0, GLA_CHUNK, state, direct)
            return tuple(state)

        state = lax.fori_loop(0, n_chunks // GLA_UNROLL, body, tuple(state))
        for h in range(N_HEADS):
            s_ref[0, h] = state[h].T

    g_main = g_ref[N_META:rows, :].reshape(n_chunks, GLA_CHUNK, WIDTH)
    decay = jnp.maximum(jnp.max(-jnp.sum(g_main, axis=1)),
                        jnp.max(-jnp.sum(g_ref[0:N_META, :], axis=0)))
    safe = decay <= GLA_SAFE_DECAY
    pl.when(safe)(functools.partial(run_all, False))
    pl.when(jnp.logical_not(safe))(functools.partial(run_all, True))


def _gla_prompt(qa, ka, va, lf, batch, seq_rows):
    blk = pl.BlockSpec((seq_rows, WIDTH), lambda b: (b, 0))
    return pl.pallas_call(
        _gla_prompt_kernel,
        grid=(batch,),
        in_specs=[blk] * 4,
        out_specs=[blk, pl.BlockSpec((1, N_HEADS, HEAD_DIM, HEAD_DIM), lambda b: (b, 0, 0, 0))],
        out_shape=[jax.ShapeDtypeStruct((batch * seq_rows, WIDTH), MXU_DTYPE),
                   jax.ShapeDtypeStruct((batch, N_HEADS, HEAD_DIM, HEAD_DIM), F32)],
        compiler_params=_params("parallel"),
        name="gla_prompt",
    )(qa, ka, va, lf)


def _gla_sample_kernel(q_ref, k_ref, v_ref, g_ref, s0_ref, o_ref, s_ref, *, steps):
    q = q_ref[...].astype(F32)
    k = k_ref[...].astype(F32)
    v = v_ref[...].astype(F32)
    g = g_ref[...]
    row = lax.broadcasted_iota(jnp.int32, q.shape, 0)

    def run_all(direct):
        out = jnp.zeros(q.shape, F32)
        for bi in range(ROWS_PER_STEP // steps):
            mine = (row >= bi * steps) & (row < (bi + 1) * steps)
            state = [s0_ref[bi, h].T for h in range(N_HEADS)]
            o, state = _gla_chunk(q, jnp.where(mine, k, 0.0), v, jnp.where(mine, g, 0.0),
                                  state, direct)
            out = jnp.where(mine, o, out)
            for h in range(N_HEADS):
                s_ref[bi, h] = state[h].T
        o_ref[...] = out.astype(o_ref.dtype)

    safe = jnp.max(-jnp.sum(g, axis=0)) <= GLA_SAFE_DECAY
    pl.when(safe)(functools.partial(run_all, False))
    pl.when(jnp.logical_not(safe))(functools.partial(run_all, True))


def _gla_sample(qa, ka, va, lf, states, layer, row0, dec_batch, steps):
    per = ROWS_PER_STEP // steps
    blk = pl.BlockSpec((ROWS_PER_STEP, WIDTH), lambda i: (row0 // ROWS_PER_STEP + i, 0))
    sblk = pl.BlockSpec((per, N_HEADS, HEAD_DIM, HEAD_DIM), lambda i: (i, 0, 0, 0))
    s0blk = pl.BlockSpec((None, per, N_HEADS, HEAD_DIM, HEAD_DIM), lambda i: (layer, i, 0, 0, 0))
    return pl.pallas_call(
        functools.partial(_gla_sample_kernel, steps=steps),
        grid=(dec_batch // per,),
        in_specs=[blk] * 4 + [s0blk],
        out_specs=[pl.BlockSpec((ROWS_PER_STEP, WIDTH), lambda i: (i, 0)), sblk],
        out_shape=[jax.ShapeDtypeStruct((dec_batch * steps, WIDTH), MXU_DTYPE),
                   jax.ShapeDtypeStruct(states.shape[1:], F32)],
        compiler_params=_params("parallel"),
        name="gla_sample",
    )(qa, ka, va, lf, states)


def _diff_finish(o1, o2, lam, scale, axis):
    d = o1 - lam * o2
    return d * lax.rsqrt(jnp.mean(d * d, axis=axis, keepdims=True) + RMS_EPS) * scale


def _attn_prompt_kernel(sc_ref, q_ref, k_ref, v_ref, o_ref, kbf_ref, vt_ref, s_ref, p_ref):
    rows = q_ref.shape[0]
    n_full = rows // ATT_TILE
    rem = rows - n_full * ATT_TILE
    lam = sc_ref[0]
    out_scale = sc_ref[1]

    kbf_ref[0:rows, :] = k_ref[...].astype(MXU_DTYPE)
    kbf_ref[rows:, :] = jnp.zeros((kbf_ref.shape[0] - rows, HEAD_DIM), MXU_DTYPE)
    for i in range(n_full):
        vt_ref[0:HEAD_DIM, i * ATT_TILE:(i + 1) * ATT_TILE] = (
            v_ref[i * ATT_TILE:(i + 1) * ATT_TILE, :].astype(F32).T.astype(MXU_DTYPE))
    v_tail = jnp.concatenate(
        [v_ref[n_full * ATT_TILE:rows, :].astype(F32),
         jnp.zeros((ATT_TILE - rem, HEAD_DIM), F32)], axis=0)
    vt_ref[0:HEAD_DIM, n_full * ATT_TILE:] = v_tail.T.astype(MXU_DTYPE)
    vt_ref[HEAD_DIM:, :] = jnp.ones((ONES_ROWS, vt_ref.shape[1]), MXU_DTYPE)

    def q_tile(q0, tq, out_rows, slot):
        q_t = q_ref[q0:q0 + tq, :].astype(F32).T * (HALF_DIM ** -0.5 * LOG2_E)
        comp = lax.broadcasted_iota(jnp.int32, q_t.shape, 0) < HALF_DIM
        q_both = jnp.concatenate([jnp.where(comp, q_t, 0.0), jnp.where(comp, 0.0, q_t)],
                                 axis=1).astype(MXU_DTYPE)
        w = 2 * tq
        visible = q0 // ATT_TILE
        tiles = [(i * ATT_TILE, ATT_TILE, i >= visible)
                 for i in range(n_full) if i * ATT_TILE < q0 + tq]
        if n_full * ATT_TILE < q0 + tq:
            tiles.append((n_full * ATT_TILE, ATT_TAIL, True))
        k_tot = tiles[-1][0] + tiles[-1][1]

        m = None
        for kv0, tk, masked in tiles:
            s = _dot(kbf_ref[kv0:kv0 + tk, :], q_both)
            if masked:
                kpos = kv0 + lax.broadcasted_iota(jnp.int32, s.shape, 0)
                qcol = lax.broadcasted_iota(jnp.int32, s.shape, 1)
                qpos = q0 + jnp.where(qcol >= tq, qcol - tq, qcol)
                s = jnp.where(kpos <= qpos, s, NEG_BIG)
            s_ref[slot, kv0:kv0 + tk, 0:w] = s
            tile_max = jnp.max(s, axis=0, keepdims=True)
            m = tile_max if m is None else jnp.maximum(m, tile_max)
        for kv0, tk, _ in tiles:
            p_ref[slot, kv0:kv0 + tk, 0:w] = jnp.exp2(
                s_ref[slot, kv0:kv0 + tk, 0:w] - m).astype(MXU_DTYPE)
        acc = _dot(vt_ref[:, 0:k_tot], p_ref[slot, 0:k_tot, 0:w])
        o = acc[:HEAD_DIM] / acc[HEAD_DIM:HEAD_DIM + 1]
        d = _diff_finish(o[:, :tq], o[:, tq:], lam, out_scale, 0)
        o_ref[q0 + tq - out_rows:q0 + tq, :] = d.T[tq - out_rows:, :].astype(o_ref.dtype)

    for j in range(n_full):
        q_tile(j * ATT_TILE, ATT_TILE, ATT_TILE, j % 2)
    if rem:
        q_tile(rows - ATT_TAIL, ATT_TAIL, rem, n_full % 2)


def _attn_prompt(scalars, qb, kb, vb, batch, seq_rows):
    assert seq_rows >= ATT_TILE and seq_rows % ATT_TILE <= ATT_TAIL
    n_full = seq_rows // ATT_TILE
    kv_rows = n_full * ATT_TILE + ATT_TAIL
    blk = pl.BlockSpec((seq_rows, HEAD_DIM), lambda b, h: (b, h))
    return pl.pallas_call(
        _attn_prompt_kernel,
        grid=(batch, N_HEADS),
        in_specs=[pl.BlockSpec(memory_space=pltpu.SMEM), blk, blk, blk],
        out_specs=blk,
        out_shape=jax.ShapeDtypeStruct((batch * seq_rows, WIDTH), MXU_DTYPE),
        scratch_shapes=[pltpu.VMEM((kv_rows, HEAD_DIM), MXU_DTYPE),
                        pltpu.VMEM((HEAD_DIM + ONES_ROWS, (n_full + 1) * ATT_TILE), MXU_DTYPE),
                        pltpu.VMEM((2, kv_rows, 2 * ATT_TILE), F32),
                        pltpu.VMEM((2, kv_rows, 2 * ATT_TILE), MXU_DTYPE)],
        compiler_params=_params("parallel", "parallel"),
        name="attn_prompt",
    )(scalars, qb, kb, vb)


def _attn_sample_kernel(pt_ref, sc_ref, q_ref, kn_ref, vn_ref, ck_ref, cv_ref, o_ref,
                        kbuf, vbuf, sem, *, layer, n_pages, dec_batch, steps):
    step_id = pl.program_id(0)
    per = ROWS_PER_STEP // steps
    lam = sc_ref[0]
    out_scale = sc_ref[1]

    def copies(b, slot):
        out = []
        for p in range(n_pages):
            page = pt_ref[b * n_pages + p]
            dst = pl.ds(p * PAGE * N_HEADS, PAGE * N_HEADS)
            out.append(pltpu.make_async_copy(ck_ref.at[layer, page], kbuf.at[slot, dst],
                                             sem.at[slot, 0]))
            out.append(pltpu.make_async_copy(cv_ref.at[layer, page], vbuf.at[slot, dst],
                                             sem.at[slot, 1]))
        return out

    def heads_to_lanes(buf, slot):
        past = n_pages * PAGE
        return jnp.concatenate(
            [buf[slot, pl.ds(h, past, stride=N_HEADS), :].astype(MXU_DTYPE)
             for h in range(N_HEADS)], axis=1)

    @pl.when(step_id == 0)
    def _():
        for c in copies(0, 0):
            c.start()

    q_all = q_ref[...].astype(F32) * (HALF_DIM ** -0.5)
    k_new = kn_ref[...].astype(MXU_DTYPE)
    v_new = vn_ref[...].astype(MXU_DTYPE)
    n_q = 2 * N_HEADS * 8
    g_row = lax.broadcasted_iota(jnp.int32, (n_q, WIDTH), 0) // 8
    g_lane = lax.broadcasted_iota(jnp.int32, (n_q, WIDTH), 1) // HALF_DIM
    new_row = lax.broadcasted_iota(jnp.int32, (n_q, ROWS_PER_STEP), 0) % 8
    new_col = lax.broadcasted_iota(jnp.int32, (n_q, ROWS_PER_STEP), 1)
    results = []
    for bi in range(per):
        b = step_id * per + bi
        slot = bi % 2

        @pl.when(b + 1 < dec_batch)
        def _():
            for c in copies(b + 1, 1 - slot):
                c.start()

        for c in copies(b, slot):
            c.wait()

        q8 = q_all[8 * (bi * steps // 8):8 * (bi * steps // 8) + 8, :]
        q_bd = jnp.where(g_row == g_lane, jnp.concatenate([q8] * (2 * N_HEADS), axis=0),
                         0.0).astype(MXU_DTYPE)
        s_past = _dot_nt(q_bd, heads_to_lanes(kbuf, slot))
        s_new = _dot_nt(q_bd, k_new)
        new_ok = (new_col // steps == bi) & (new_col % steps <= new_row % steps)
        s_new = jnp.where(new_ok, s_new, NEG_BIG)
        m = jnp.maximum(jnp.max(s_past, axis=1, keepdims=True),
                        jnp.max(s_new, axis=1, keepdims=True))
        p_past = jnp.exp(s_past - m)
        p_new = jnp.exp(s_new - m)
        l = jnp.sum(p_past, axis=1, keepdims=True) + jnp.sum(p_new, axis=1, keepdims=True)
        o = (_dot(p_past.astype(MXU_DTYPE), heads_to_lanes(vbuf, slot))
             + _dot(p_new.astype(MXU_DTYPE), v_new)) / l
        heads = []
        for h in range(N_HEADS):
            sl = slice(h * HEAD_DIM, (h + 1) * HEAD_DIM)
            heads.append(_diff_finish(o[16 * h:16 * h + 8, sl], o[16 * h + 8:16 * h + 16, sl],
                                      lam, out_scale, 1))
        results.append(jnp.concatenate(heads, axis=1))

    row8 = lax.broadcasted_iota(jnp.int32, (8, WIDTH), 0)
    merged = []
    for pair in range(ROWS_PER_STEP // 8):
        acc = results[pair * (8 // steps)]
        for j in range(1, 8 // steps):
            acc = jnp.where(row8 // steps == j, results[pair * (8 // steps) + j], acc)
        merged.append(acc)
    o_ref[...] = jnp.concatenate(merged, axis=0).astype(o_ref.dtype)


def _attn_sample(page_table, scalars, qb, kb, vb, cache_k, cache_v, layer, row0, dec_batch, steps):
    assert 8 % steps == 0
    n_pages = page_table.shape[1]
    per = ROWS_PER_STEP // steps
    blk = pl.BlockSpec((ROWS_PER_STEP, WIDTH), lambda i, pt: (row0 // ROWS_PER_STEP + i, 0))
    grid_spec = pltpu.PrefetchScalarGridSpec(
        num_scalar_prefetch=1,
        grid=(dec_batch // per,),
        in_specs=[pl.BlockSpec(memory_space=pltpu.SMEM), blk, blk, blk,
                  pl.BlockSpec(memory_space=pl.ANY), pl.BlockSpec(memory_space=pl.ANY)],
        out_specs=pl.BlockSpec((ROWS_PER_STEP, WIDTH), lambda i, pt: (i, 0)),
        scratch_shapes=[pltpu.VMEM((2, n_pages * PAGE * N_HEADS, HEAD_DIM), F32),
                        pltpu.VMEM((2, n_pages * PAGE * N_HEADS, HEAD_DIM), F32),
                        pltpu.SemaphoreType.DMA((2, 2))],
    )
    return pl.pallas_call(
        functools.partial(_attn_sample_kernel, layer=layer, n_pages=n_pages,
                          dec_batch=dec_batch, steps=steps),
        grid_spec=grid_spec,
        out_shape=jax.ShapeDtypeStruct((dec_batch * steps, WIDTH), MXU_DTYPE),
        compiler_params=_params("arbitrary"),
        name="attn_sample",
    )(page_table.reshape(-1), scalars, qb, kb, vb, cache_k, cache_v)


def kernel(x_prompt, x_sample, cache_k, cache_v, state_hgrn, page_table, meta_tokens, ffn1_pre_norm, ffn1_post_norm, ffn1_w_gate, ffn1_w_up, ffn1_w_down, mix_pre_norm, mix_post_norm, w_in, hgrn_lb_logits, hgrn_norm_w, lambda_q1, lambda_k1, lambda_q2, lambda_k2, w_branch_a, w_branch_b, w_out, ffn2_pre_norm, ffn2_post_norm, ffn2_w_gate, ffn2_w_up, ffn2_w_down):
    batch, seq, d = x_prompt.shape
    dec_batch, steps = x_sample.shape[:2]
    depth = w_in.shape[0]
    seq_rows = N_META + seq
    n_prompt = batch * seq_rows
    n_sample = dec_batch * steps
    n = n_prompt + n_sample
    assert n_prompt % ROWS_PER_STEP == 0 and ROWS_PER_STEP % steps == 0
    assert (seq_rows - N_META) % (GLA_CHUNK * GLA_UNROLL) == 0
    tm = _row_tile(n, ROW_TILE_TARGET)

    lbs = jnp.cumsum(jax.nn.softmax(hgrn_lb_logits.astype(F32), axis=0), axis=0)
    lbs = lbs - lbs[0:1]
    lam_init = jnp.asarray([0.8 - 0.6 * math.exp(-0.3 * l) for l in range(depth)], F32)
    lam = (jnp.exp(jnp.sum(lambda_q1.astype(F32) * lambda_k1.astype(F32), axis=-1))
           - jnp.exp(jnp.sum(lambda_q2.astype(F32) * lambda_k2.astype(F32), axis=-1)) + lam_init)
    attn_scalars = jnp.stack([lam, 1.0 - lam_init], axis=1)

    meta = jnp.broadcast_to(meta_tokens[None].astype(x_prompt.dtype), (batch, N_META, d))
    h = jnp.concatenate([jnp.concatenate([meta, x_prompt], axis=1).reshape(n_prompt, d),
                         x_sample.reshape(n_sample, d)], axis=0)
    cache_k = cache_k.reshape(cache_k.shape[:2] + (PAGE * N_HEADS, HEAD_DIM))
    cache_v = cache_v.reshape(cache_v.shape[:2] + (PAGE * N_HEADS, HEAD_DIM))
    n_front = 7 * WIDTH

    def cast(w):
        return w.astype(MXU_DTYPE)

    def vec(w):
        return w.reshape(1, -1).astype(F32)

    k_p, v_p, st_p, k_s, v_s, st_s = [], [], [], [], [], []
    for l in range(depth):
        h = _ffn_half(h, vec(ffn1_pre_norm[l]), vec(ffn1_post_norm[l]), cast(ffn1_w_gate[l]),
                      cast(ffn1_w_up[l]), cast(ffn1_w_down[l]), tm)
        w_l = cast(w_in[l])
        qa, ka, va, lf, zg, qb, kb, vb, k_out, v_out = _mixer_front(
            h, vec(mix_pre_norm[l]), w_l[:, :n_front], vec(lbs[l]), tm)
        oa_p, s_p = _gla_prompt(qa, ka, va, lf, batch, seq_rows)
        ob_p = _attn_prompt(attn_scalars[l], qb, kb, vb, batch, seq_rows)
        oa_s, s_s = _gla_sample(qa, ka, va, lf, state_hgrn, l, n_prompt,
                                dec_batch, steps)
        ob_s = _attn_sample(page_table, attn_scalars[l], qb, kb, vb, cache_k, cache_v, l,
                            n_prompt, dec_batch, steps)
        h = _mixer_back(h, jnp.concatenate([oa_p, oa_s], axis=0), zg,
                        jnp.concatenate([ob_p, ob_s], axis=0), vec(mix_pre_norm[l]),
                        vec(mix_post_norm[l]), vec(hgrn_norm_w[l]), w_l[:, n_front:],
                        cast(w_branch_a[l]), cast(w_branch_b[l]), cast(w_out[l]), tm)
        h = _ffn_half(h, vec(ffn2_pre_norm[l]), vec(ffn2_post_norm[l]), cast(ffn2_w_gate[l]),
                      cast(ffn2_w_up[l]), cast(ffn2_w_down[l]), tm)
        split = n_prompt * N_HEADS
        k_p.append(k_out[:split].reshape(batch, seq_rows, N_HEADS, HEAD_DIM))
        v_p.append(v_out[:split].reshape(batch, seq_rows, N_HEADS, HEAD_DIM))
        st_p.append(s_p.astype(state_hgrn.dtype))
        k_s.append(k_out[split:].reshape(dec_batch, steps, N_HEADS, HEAD_DIM))
        v_s.append(v_out[split:].reshape(dec_batch, steps, N_HEADS, HEAD_DIM))
        st_s.append(s_s.astype(state_hgrn.dtype))
    y_prompt = h[:n_prompt].reshape(batch, seq_rows, d)[:, N_META:]
    y_sample = h[n_prompt:].reshape(dec_batch, steps, d)
    return (y_prompt, y_sample, jnp.stack(k_p), jnp.stack(v_p), jnp.stack(st_p),
            jnp.stack(k_s), jnp.stack(v_s), jnp.stack(st_s))
```

```python
import functools
import math

import jax
import jax.numpy as jnp
from jax import lax
from jax.experimental import pallas as pl
from jax.experimental.pallas import tpu as pltpu

F32 = jnp.float32
MXU_DTYPE = jnp.bfloat16
RMS_EPS = 1e-6
N_META = 16
N_HEADS = 4
HEAD_DIM = 128
HALF_DIM = HEAD_DIM // 2
WIDTH = N_HEADS * HEAD_DIM
PAGE = 128
GLA_CHUNK = 64
GLA_UNROLL = 4
GLA_SAFE_DECAY = 80.0
ATT_TILE = 256
ATT_TAIL = 128
ONES_ROWS = 16
LOG2_E = 1.4426950408889634
ROWS_PER_STEP = 16
FF_CHUNK = 1024
ROW_TILE_TARGET = 896
NEG_BIG = -1e30
VMEM_LIMIT = 56 * 1024 * 1024


def _rms(x):
    return x * lax.rsqrt(jnp.mean(x * x, axis=-1, keepdims=True) + RMS_EPS)


def _dot(a, b):
    return jnp.dot(a, b, preferred_element_type=F32)


def _dot_nt(a, b):
    return lax.dot_general(a, b, (((1,), (1,)), ((), ())), preferred_element_type=F32)


def _dot_tn(a, b):
    return lax.dot_general(a, b, (((0,), (0,)), ((), ())), preferred_element_type=F32)


def _silu(x):
    return x * jax.nn.sigmoid(x)


def _row_tile(n, target, align=16):
    best = None
    for t in range(align, min(n, target) + 1, align):
        if n % t == 0:
            best = t
    assert best is not None, (n, target)
    return best


def _chunks(total, size):
    return tuple((c, min(c + size, total)) for c in range(0, total, size))


def _params(*sem):
    return pltpu.CompilerParams(dimension_semantics=sem, vmem_limit_bytes=VMEM_LIMIT)


def _const_spec(shape):
    return pl.BlockSpec(shape, lambda *_: (0,) * len(shape), pipeline_mode=pl.Buffered(1))


def _ffn_kernel(h_ref, pre_ref, post_ref, wg_ref, wu_ref, wd_ref, o_ref, *, chunks):
    x = h_ref[...]
    xn = (_rms(x) * pre_ref[...]).astype(MXU_DTYPE)
    y = None
    for c0, c1 in chunks:
        g = _dot(xn, wg_ref[:, c0:c1])
        u = _dot(xn, wu_ref[:, c0:c1])
        part = _dot((_silu(g) * u).astype(MXU_DTYPE), wd_ref[c0:c1, :])
        y = part if y is None else y + part
    o_ref[...] = x + 0.5 * (_rms(y) * post_ref[...])


def _ffn_half(h, pre, post, wg, wu, wd, tm):
    n, d = h.shape
    f = wg.shape[1]
    row = pl.BlockSpec((tm, d), lambda i: (i, 0))
    return pl.pallas_call(
        functools.partial(_ffn_kernel, chunks=_chunks(f, FF_CHUNK)),
        grid=(n // tm,),
        in_specs=[row, _const_spec((1, d)), _const_spec((1, d)),
                  _const_spec((d, f)), _const_spec((d, f)), _const_spec((f, d))],
        out_specs=row,
        out_shape=jax.ShapeDtypeStruct((n, d), F32),
        compiler_params=_params("parallel"),
        name="ffn_half",
    )(h, pre, post, wg, wu, wd)


def _front_kernel(h_ref, pre_ref, w_ref, lb_ref,
                  qa_ref, ka_ref, va_ref, lf_ref, zg_ref, qb_ref, kb_ref, vb_ref,
                  kout_ref, vout_ref):
    x = h_ref[...]
    xn = (_rms(x) * pre_ref[...]).astype(MXU_DTYPE)

    def proj(i):
        return _dot(xn, w_ref[:, i * WIDTH:(i + 1) * WIDTH])

    qa_ref[...] = proj(0).astype(qa_ref.dtype)
    zf = proj(1)
    lb = lb_ref[...]
    e = jnp.exp(-jnp.abs(zf))
    r = 1.0 / (1.0 + e)
    sig_pos = jnp.where(zf >= 0, r, e * r)
    sig_neg = jnp.where(zf >= 0, e * r, r)
    lf_ref[...] = jnp.log(lb + (1.0 - lb) * sig_pos)
    ka_ref[...] = ((1.0 - lb) * sig_neg).astype(ka_ref.dtype)
    va_ref[...] = proj(2).astype(va_ref.dtype)
    zg_ref[...] = proj(3).astype(zg_ref.dtype)
    qb_ref[...] = proj(4).astype(qb_ref.dtype)
    tm = x.shape[0]
    for src, dst_ref, out_ref in ((proj(5), kb_ref, kout_ref), (proj(6), vb_ref, vout_ref)):
        dst_ref[...] = src.astype(dst_ref.dtype)
        for hd in range(N_HEADS):
            out_ref[pl.ds(hd, tm, stride=N_HEADS), :] = src[:, hd * HEAD_DIM:(hd + 1) * HEAD_DIM]


def _mixer_front(h, pre, w_front, lb, tm):
    n, d = h.shape
    row = pl.BlockSpec((tm, d), lambda i: (i, 0))
    col = pl.BlockSpec((tm, WIDTH), lambda i: (i, 0))
    half = jax.ShapeDtypeStruct((n, WIDTH), MXU_DTYPE)
    full = jax.ShapeDtypeStruct((n, WIDTH), F32)
    per_head = jax.ShapeDtypeStruct((n * N_HEADS, HEAD_DIM), F32)
    per_head_blk = pl.BlockSpec((tm * N_HEADS, HEAD_DIM), lambda i: (i, 0))
    return pl.pallas_call(
        _front_kernel,
        grid=(n // tm,),
        in_specs=[row, _const_spec((1, d)), _const_spec((d, 7 * WIDTH)),
                  _const_spec((1, WIDTH))],
        out_specs=[col] * 8 + [per_head_blk] * 2,
        out_shape=[half, half, half, full, half, half, half, half, per_head, per_head],
        compiler_params=_params("parallel"),
        name="mixer_front",
    )(h, pre, w_front, lb)


def _back_kernel(h_ref, oa_ref, zg_ref, ob_ref, pre_ref, post_ref, nw_ref,
                 wgate_ref, wa_ref, wb_ref, wo_ref, o_ref):
    x = h_ref[...]
    d = x.shape[1]
    xn = (_rms(x) * pre_ref[...]).astype(MXU_DTYPE)
    oa = oa_ref[...].astype(F32)
    heads = [_rms(oa[:, h * HEAD_DIM:(h + 1) * HEAD_DIM]) * nw_ref[...]
             for h in range(N_HEADS)]
    ya = (jnp.concatenate(heads, axis=1) * _silu(zg_ref[...].astype(F32))).astype(MXU_DTYPE)
    gate_a = jax.nn.sigmoid(_dot(xn, wgate_ref[:, :d]))
    gate_b = jax.nn.sigmoid(_dot(xn, wgate_ref[:, d:]))
    merged = gate_a * _dot(ya, wa_ref[...]) + gate_b * _dot(ob_ref[...], wb_ref[...])
    out = _dot(merged.astype(MXU_DTYPE), wo_ref[...])
    o_ref[...] = x + _rms(out) * post_ref[...]


def _mixer_back(h, oa, zg, ob, pre, post, nw, w_gate, wa, wb, wo, tm):
    n, d = h.shape
    row = pl.BlockSpec((tm, d), lambda i: (i, 0))
    col = pl.BlockSpec((tm, WIDTH), lambda i: (i, 0))
    return pl.pallas_call(
        _back_kernel,
        grid=(n // tm,),
        in_specs=[row, col, col, col, _const_spec((1, d)), _const_spec((1, d)),
                  _const_spec((1, HEAD_DIM)), _const_spec((d, 2 * d)),
                  _const_spec((WIDTH, d)), _const_spec((WIDTH, d)), _const_spec((d, d))],
        out_specs=row,
        out_shape=jax.ShapeDtypeStruct((n, d), F32),
        compiler_params=_params("parallel"),
        name="mixer_back",
    )(h, oa, zg, ob, pre, post, nw, w_gate, wa, wb, wo)


def _cumsum_rows(x):
    rows = x.shape[0]
    row = lax.broadcasted_iota(jnp.int32, x.shape, 0)
    shift = 1
    while shift < rows:
        x = x + jnp.where(row >= shift, pltpu.roll(x, shift, 0), 0.0)
        shift *= 2
    return x


def _scores_direct(q, k, cum):
    c = q.shape[0]
    col = lax.broadcasted_iota(jnp.int32, (c, c), 1)
    row = lax.broadcasted_iota(jnp.int32, (c, 1), 0)

    def body(s, acc):
        pick = row == s
        ks = jnp.sum(jnp.where(pick, k, 0.0), axis=0, keepdims=True)
        cs = jnp.sum(jnp.where(pick, cum, 0.0), axis=0, keepdims=True)
        t = q * ks * jnp.exp(jnp.minimum(cum - cs, 0.0))
        return jnp.where(col == s, jnp.sum(t, axis=1, keepdims=True), acc)

    return lax.fori_loop(0, c, body, jnp.zeros((c, c), F32))


def _gla_chunk(q, k, v, g, state, direct):
    c = q.shape[0]
    cum = _cumsum_rows(g)
    last = cum[c - 1:c, :]
    q_dec = (q * jnp.exp(cum)).astype(MXU_DTYPE)
    k_end = (k * jnp.exp(last - cum)).astype(MXU_DTYPE)
    v_mx = v.astype(MXU_DTYPE)
    state_decay = jnp.exp(last)
    if not direct:
        k_inv = (k * jnp.exp(-cum)).astype(MXU_DTYPE)
    tril = (lax.broadcasted_iota(jnp.int32, (c, c), 1)
            <= lax.broadcasted_iota(jnp.int32, (c, c), 0))
    outs, new_state = [], []
    for h in range(N_HEADS):
        sl = slice(h * HEAD_DIM, (h + 1) * HEAD_DIM)
        if direct:
            scores = _scores_direct(q[:, sl], k[:, sl], cum[:, sl])
        else:
            scores = _dot_nt(q_dec[:, sl], k_inv[:, sl])
        scores = jnp.where(tril, scores, 0.0).astype(MXU_DTYPE)
        outs.append(_dot_nt(q_dec[:, sl], state[h].astype(MXU_DTYPE)) + _dot(scores, v_mx[:, sl]))
        new_state.append(state[h] * state_decay[:, sl] + _dot_tn(v_mx[:, sl], k_end[:, sl]))
    return jnp.concatenate(outs, axis=1), new_state


def _gla_prompt_kernel(q_ref, k_ref, v_ref, g_ref, o_ref, s_ref):
    rows = q_ref.shape[0]
    n_chunks = (rows - N_META) // GLA_CHUNK

    def run(r0, c, state, direct):
        sl = pl.ds(r0, c)
        o, state = _gla_chunk(q_ref[sl, :].astype(F32), k_ref[sl, :].astype(F32),
                              v_ref[sl, :].astype(F32), g_ref[sl, :], state, direct)
        o_ref[sl, :] = o.astype(o_ref.dtype)
        return state

    def run_all(direct):
        state = [jnp.zeros((HEAD_DIM, HEAD_DIM), F32)] * N_HEADS
        state = run(0, N_META, state, direct)

        def body(i, state):
            state = list(state)
            for u in range(GLA_UNROLL):
                r0 = pl.multiple_of(N_META + (i * GLA_UNROLL + u) * GLA_CHUNK, 16)
                state = run(r0, GLA_CHUNK, state, direct)
            return tuple(state)

        state = lax.fori_loop(0, n_chunks // GLA_UNROLL, body, tuple(state))
        for h in range(N_HEADS):
            s_ref[0, h] = state[h].T

    g_main = g_ref[N_META:rows, :].reshape(n_chunks, GLA_CHUNK, WIDTH)
    decay = jnp.maximum(jnp.max(-jnp.sum(g_main, axis=1)),
                        jnp.max(-jnp.sum(g_ref[0:N_META, :], axis=0)))
    safe = decay <= GLA_SAFE_DECAY
    pl.when(safe)(functools.partial(run_all, False))
    pl.when(jnp.logical_not(safe))(functools.partial(run_all, True))


def _gla_prompt(qa, ka, va, lf, batch, seq_rows):
    blk = pl.BlockSpec((seq_rows, WIDTH), lambda b: (b, 0))
    return pl.pallas_call(
        _gla_prompt_kernel,
        grid=(batch,),
        in_specs=[blk] * 4,
        out_specs=[blk, pl.BlockSpec((1, N_HEADS, HEAD_DIM, HEAD_DIM), lambda b: (b, 0, 0, 0))],
        out_shape=[jax.ShapeDtypeStruct((batch * seq_rows, WIDTH), MXU_DTYPE),
                   jax.ShapeDtypeStruct((batch, N_HEADS, HEAD_DIM, HEAD_DIM), F32)],
        compiler_params=_params("parallel"),
        name="gla_prompt",
    )(qa, ka, va, lf)


def _gla_sample_kernel(q_ref, k_ref, v_ref, g_ref, s0_ref, o_ref, s_ref, *, steps):
    q = q_ref[...].astype(F32)
    k = k_ref[...].astype(F32)
    v = v_ref[...].astype(F32)
    g = g_ref[...]
    row = lax.broadcasted_iota(jnp.int32, q.shape, 0)

    def run_all(direct):
        out = jnp.zeros(q.shape, F32)
        for bi in range(ROWS_PER_STEP // steps):
            mine = (row >= bi * steps) & (row < (bi + 1) * steps)
            state = [s0_ref[bi, h].T for h in range(N_HEADS)]
            o, state = _gla_chunk(q, jnp.where(mine, k, 0.0), v, jnp.where(mine, g, 0.0),
                                  state, direct)
            out = jnp.where(mine, o, out)
            for h in range(N_HEADS):
                s_ref[bi, h] = state[h].T
        o_ref[...] = out.astype(o_ref.dtype)

    safe = jnp.max(-jnp.sum(g, axis=0)) <= GLA_SAFE_DECAY
    pl.when(safe)(functools.partial(run_all, False))
    pl.when(jnp.logical_not(safe))(functools.partial(run_all, True))


def _gla_sample(qa, ka, va, lf, states, layer, row0, dec_batch, steps):
    per = ROWS_PER_STEP // steps
    blk = pl.BlockSpec((ROWS_PER_STEP, WIDTH), lambda i: (row0 // ROWS_PER_STEP + i, 0))
    sblk = pl.BlockSpec((per, N_HEADS, HEAD_DIM, HEAD_DIM), lambda i: (i, 0, 0, 0))
    s0blk = pl.BlockSpec((None, per, N_HEADS, HEAD_DIM, HEAD_DIM), lambda i: (layer, i, 0, 0, 0))
    return pl.pallas_call(
        functools.partial(_gla_sample_kernel, steps=steps),
        grid=(dec_batch // per,),
        in_specs=[blk] * 4 + [s0blk],
        out_specs=[pl.BlockSpec((ROWS_PER_STEP, WIDTH), lambda i: (i, 0)), sblk],
        out_shape=[jax.ShapeDtypeStruct((dec_batch * steps, WIDTH), MXU_DTYPE),
                   jax.ShapeDtypeStruct(states.shape[1:], F32)],
        compiler_params=_params("parallel"),
        name="gla_sample",
    )(qa, ka, va, lf, states)


def _diff_finish(o1, o2, lam, scale, axis):
    d = o1 - lam * o2
    return d * lax.rsqrt(jnp.mean(d * d, axis=axis, keepdims=True) + RMS_EPS) * scale


def _attn_kernel(pt_ref, sc_ref, q_ref, k_ref, v_ref, qs_ref, kn_ref, vn_ref, ck_ref, cv_ref,
                 o_ref, os_ref, kbf_ref, vt_ref, s_ref, p_ref, kbuf, vbuf, sem,
                 *, layer, n_pages, dec_batch, steps):
    rows = q_ref.shape[0]
    n_full = rows // ATT_TILE
    rem = rows - n_full * ATT_TILE
    lam = sc_ref[0]
    out_scale = sc_ref[1]
    step_id = pl.program_id(0) * pl.num_programs(1) + pl.program_id(1)
    per = ROWS_PER_STEP // steps

    def copies(b, slot):
        out = []
        for p in range(n_pages):
            page = pt_ref[b * n_pages + p]
            dst = pl.ds(p * PAGE * N_HEADS, PAGE * N_HEADS)
            out.append(pltpu.make_async_copy(ck_ref.at[layer, page], kbuf.at[slot, dst],
                                             sem.at[slot, 0]))
            out.append(pltpu.make_async_copy(cv_ref.at[layer, page], vbuf.at[slot, dst],
                                             sem.at[slot, 1]))
        return out

    @pl.when(step_id == 0)
    def _():
        for c in copies(0, 0):
            c.start()

    kbf_ref[0:rows, :] = k_ref[...].astype(MXU_DTYPE)
    kbf_ref[rows:, :] = jnp.zeros((kbf_ref.shape[0] - rows, HEAD_DIM), MXU_DTYPE)
    for i in range(n_full):
        vt_ref[0:HEAD_DIM, i * ATT_TILE:(i + 1) * ATT_TILE] = (
            v_ref[i * ATT_TILE:(i + 1) * ATT_TILE, :].astype(F32).T.astype(MXU_DTYPE))
    v_tail = jnp.concatenate(
        [v_ref[n_full * ATT_TILE:rows, :].astype(F32),
         jnp.zeros((ATT_TILE - rem, HEAD_DIM), F32)], axis=0)
    vt_ref[0:HEAD_DIM, n_full * ATT_TILE:] = v_tail.T.astype(MXU_DTYPE)
    vt_ref[HEAD_DIM:, :] = jnp.ones((ONES_ROWS, vt_ref.shape[1]), MXU_DTYPE)

    def q_tile(q0, tq, out_rows, slot):
        q_t = q_ref[q0:q0 + tq, :].astype(F32).T * (HALF_DIM ** -0.5 * LOG2_E)
        comp = lax.broadcasted_iota(jnp.int32, q_t.shape, 0) < HALF_DIM
        q_both = jnp.concatenate([jnp.where(comp, q_t, 0.0), jnp.where(comp, 0.0, q_t)],
                                 axis=1).astype(MXU_DTYPE)
        w = 2 * tq
        visible = q0 // ATT_TILE
        tiles = [(i * ATT_TILE, ATT_TILE, i >= visible)
                 for i in range(n_full) if i * ATT_TILE < q0 + tq]
        if n_full * ATT_TILE < q0 + tq:
            tiles.append((n_full * ATT_TILE, ATT_TAIL, True))
        k_tot = tiles[-1][0] + tiles[-1][1]

        m = None
        for kv0, tk, masked in tiles:
            s = _dot(kbf_ref[kv0:kv0 + tk, :], q_both)
            if masked:
                kpos = kv0 + lax.broadcasted_iota(jnp.int32, s.shape, 0)
                qcol = lax.broadcasted_iota(jnp.int32, s.shape, 1)
                qpos = q0 + jnp.where(qcol >= tq, qcol - tq, qcol)
                s = jnp.where(kpos <= qpos, s, NEG_BIG)
            s_ref[slot, kv0:kv0 + tk, 0:w] = s
            tile_max = jnp.max(s, axis=0, keepdims=True)
            m = tile_max if m is None else jnp.maximum(m, tile_max)
        for kv0, tk, _ in tiles:
            p_ref[slot, kv0:kv0 + tk, 0:w] = jnp.exp2(
                s_ref[slot, kv0:kv0 + tk, 0:w] - m).astype(MXU_DTYPE)
        acc = _dot(vt_ref[:, 0:k_tot], p_ref[slot, 0:k_tot, 0:w])
        o = acc[:HEAD_DIM] / acc[HEAD_DIM:HEAD_DIM + 1]
        d = _diff_finish(o[:, :tq], o[:, tq:], lam, out_scale, 0)
        o_ref[q0 + tq - out_rows:q0 + tq, :] = d.T[tq - out_rows:, :].astype(o_ref.dtype)

    def heads_to_lanes(buf, slot):
        past = n_pages * PAGE
        return jnp.concatenate(
            [buf[slot, pl.ds(h, past, stride=N_HEADS), :].astype(MXU_DTYPE)
             for h in range(N_HEADS)], axis=1)

    q_all = qs_ref[...].astype(F32) * (HALF_DIM ** -0.5)
    k_new = kn_ref[...].astype(MXU_DTYPE)
    v_new = vn_ref[...].astype(MXU_DTYPE)
    n_q = 2 * N_HEADS * 8
    g_row = lax.broadcasted_iota(jnp.int32, (n_q, WIDTH), 0) // 8
    g_lane = lax.broadcasted_iota(jnp.int32, (n_q, WIDTH), 1) // HALF_DIM
    new_row = lax.broadcasted_iota(jnp.int32, (n_q, ROWS_PER_STEP), 0) % 8
    new_col = lax.broadcasted_iota(jnp.int32, (n_q, ROWS_PER_STEP), 1)

    def decode(bi):
        b = step_id * per + bi
        slot = bi % 2

        @pl.when(b + 1 < dec_batch)
        def _():
            for c in copies(b + 1, 1 - slot):
                c.start()

        for c in copies(b, slot):
            c.wait()

        q8 = q_all[8 * (bi * steps // 8):8 * (bi * steps // 8) + 8, :]
        q_bd = jnp.where(g_row == g_lane, jnp.concatenate([q8] * (2 * N_HEADS), axis=0),
                         0.0).astype(MXU_DTYPE)
        s_past = _dot_nt(q_bd, heads_to_lanes(kbuf, slot))
        s_new = _dot_nt(q_bd, k_new)
        new_ok = (new_col // steps == bi) & (new_col % steps <= new_row % steps)
        s_new = jnp.where(new_ok, s_new, NEG_BIG)
        m = jnp.maximum(jnp.max(s_past, axis=1, keepdims=True),
                        jnp.max(s_new, axis=1, keepdims=True))
        p_past = jnp.exp(s_past - m)
        p_new = jnp.exp(s_new - m)
        l = jnp.sum(p_past, axis=1, keepdims=True) + jnp.sum(p_new, axis=1, keepdims=True)
        o = (_dot(p_past.astype(MXU_DTYPE), heads_to_lanes(vbuf, slot))
             + _dot(p_new.astype(MXU_DTYPE), v_new)) / l
        heads = []
        for h in range(N_HEADS):
            sl = slice(h * HEAD_DIM, (h + 1) * HEAD_DIM)
            heads.append(_diff_finish(o[16 * h:16 * h + 8, sl], o[16 * h + 8:16 * h + 16, sl],
                                      lam, out_scale, 1))
        return jnp.concatenate(heads, axis=1)

    q_tiles = [(j * ATT_TILE, ATT_TILE, ATT_TILE) for j in range(n_full)]
    if rem:
        q_tiles.append((rows - ATT_TAIL, ATT_TAIL, rem))
    share = -(-len(q_tiles) // per)
    results = []
    for bi in range(per):
        results.append(decode(bi))
        for j in range(bi * share, min((bi + 1) * share, len(q_tiles))):
            q_tile(*q_tiles[j], j % 2)

    row8 = lax.broadcasted_iota(jnp.int32, (8, WIDTH), 0)
    merged = []
    for pair in range(ROWS_PER_STEP // 8):
        acc = results[pair * (8 // steps)]
        for j in range(1, 8 // steps):
            acc = jnp.where(row8 // steps == j, results[pair * (8 // steps) + j], acc)
        merged.append(acc)
    os_ref[...] = jnp.concatenate(merged, axis=0).astype(os_ref.dtype)


def _attention(page_table, scalars, qb, kb, vb, cache_k, cache_v, layer, batch, seq_rows,
               dec_batch, steps):
    assert seq_rows >= ATT_TILE and seq_rows % ATT_TILE <= ATT_TAIL
    assert 8 % steps == 0 and dec_batch * steps == batch * N_HEADS * ROWS_PER_STEP
    n_full = seq_rows // ATT_TILE
    kv_rows = n_full * ATT_TILE + ATT_TAIL
    n_pages = page_table.shape[1]
    sample0 = batch * seq_rows // ROWS_PER_STEP
    blk = pl.BlockSpec((seq_rows, HEAD_DIM), lambda b, h, pt: (b, h))
    sblk = pl.BlockSpec((ROWS_PER_STEP, WIDTH), lambda b, h, pt: (sample0 + b * N_HEADS + h, 0))
    grid_spec = pltpu.PrefetchScalarGridSpec(
        num_scalar_prefetch=1,
        grid=(batch, N_HEADS),
        in_specs=[pl.BlockSpec(memory_space=pltpu.SMEM), blk, blk, blk, sblk, sblk, sblk,
                  pl.BlockSpec(memory_space=pl.ANY), pl.BlockSpec(memory_space=pl.ANY)],
        out_specs=[blk, pl.BlockSpec((ROWS_PER_STEP, WIDTH), lambda b, h, pt: (b * N_HEADS + h, 0))],
        scratch_shapes=[pltpu.VMEM((kv_rows, HEAD_DIM), MXU_DTYPE),
                        pltpu.VMEM((HEAD_DIM + ONES_ROWS, (n_full + 1) * ATT_TILE), MXU_DTYPE),
                        pltpu.VMEM((2, kv_rows, 2 * ATT_TILE), F32),
                        pltpu.VMEM((2, kv_rows, 2 * ATT_TILE), MXU_DTYPE),
                        pltpu.VMEM((2, n_pages * PAGE * N_HEADS, HEAD_DIM), F32),
                        pltpu.VMEM((2, n_pages * PAGE * N_HEADS, HEAD_DIM), F32),
                        pltpu.SemaphoreType.DMA((2, 2))],
    )
    return pl.pallas_call(
        functools.partial(_attn_kernel, layer=layer, n_pages=n_pages, dec_batch=dec_batch,
                          steps=steps),
        grid_spec=grid_spec,
        out_shape=[jax.ShapeDtypeStruct((batch * seq_rows, WIDTH), MXU_DTYPE),
                   jax.ShapeDtypeStruct((dec_batch * steps, WIDTH), MXU_DTYPE)],
        compiler_params=_params("arbitrary", "arbitrary"),
        name="attention",
    )(page_table.reshape(-1), scalars, qb, kb, vb, qb, kb, vb, cache_k, cache_v)


def _collect_kernel(*refs, depth, split):
    srcs, (kp, ks, vp, vs, sem) = refs[:2 * depth], refs[2 * depth:]
    copies = []
    for l in range(depth):
        for j, (src, dst_p, dst_s) in enumerate(((srcs[l], kp, ks), (srcs[depth + l], vp, vs))):
            total = src.shape[0]
            copies.append(pltpu.make_async_copy(src.at[pl.ds(0, split)], dst_p.at[l],
                                                sem.at[l, j, 0]))
            copies.append(pltpu.make_async_copy(src.at[pl.ds(split, total - split)], dst_s.at[l],
                                                sem.at[l, j, 1]))
    for c in copies:
        c.start()
    for c in copies:
        c.wait()


def _collect(k_outs, v_outs, split):
    depth = len(k_outs)
    total = k_outs[0].shape[0]
    any_spec = pl.BlockSpec(memory_space=pl.ANY)
    prompt = jax.ShapeDtypeStruct((depth, split, HEAD_DIM), F32)
    sample = jax.ShapeDtypeStruct((depth, total - split, HEAD_DIM), F32)
    return pl.pallas_call(
        functools.partial(_collect_kernel, depth=depth, split=split),
        in_specs=[any_spec] * (2 * depth),
        out_specs=[any_spec] * 4,
        out_shape=[prompt, sample, prompt, sample],
        scratch_shapes=[pltpu.SemaphoreType.DMA((depth, 2, 2))],
        name="collect_kv",
    )(*k_outs, *v_outs)


def kernel(x_prompt, x_sample, cache_k, cache_v, state_hgrn, page_table, meta_tokens, ffn1_pre_norm, ffn1_post_norm, ffn1_w_gate, ffn1_w_up, ffn1_w_down, mix_pre_norm, mix_post_norm, w_in, hgrn_lb_logits, hgrn_norm_w, lambda_q1, lambda_k1, lambda_q2, lambda_k2, w_branch_a, w_branch_b, w_out, ffn2_pre_norm, ffn2_post_norm, ffn2_w_gate, ffn2_w_up, ffn2_w_down):
    batch, seq, d = x_prompt.shape
    dec_batch, steps = x_sample.shape[:2]
    depth = w_in.shape[0]
    seq_rows = N_META + seq
    n_prompt = batch * seq_rows
    n_sample = dec_batch * steps
    n = n_prompt + n_sample
    assert n_prompt % ROWS_PER_STEP == 0 and ROWS_PER_STEP % steps == 0
    assert (seq_rows - N_META) % (GLA_CHUNK * GLA_UNROLL) == 0
    tm = _row_tile(n, ROW_TILE_TARGET)

    lbs = jnp.cumsum(jax.nn.softmax(hgrn_lb_logits.astype(F32), axis=0), axis=0)
    lbs = lbs - lbs[0:1]
    lam_init = jnp.asarray([0.8 - 0.6 * math.exp(-0.3 * l) for l in range(depth)], F32)
    lam = (jnp.exp(jnp.sum(lambda_q1.astype(F32) * lambda_k1.astype(F32), axis=-1))
           - jnp.exp(jnp.sum(lambda_q2.astype(F32) * lambda_k2.astype(F32), axis=-1)) + lam_init)
    attn_scalars = jnp.stack([lam, 1.0 - lam_init], axis=1)

    meta = jnp.broadcast_to(meta_tokens[None].astype(x_prompt.dtype), (batch, N_META, d))
    h = jnp.concatenate([jnp.concatenate([meta, x_prompt], axis=1).reshape(n_prompt, d),
                         x_sample.reshape(n_sample, d)], axis=0)
    cache_k = cache_k.reshape(cache_k.shape[:2] + (PAGE * N_HEADS, HEAD_DIM))
    cache_v = cache_v.reshape(cache_v.shape[:2] + (PAGE * N_HEADS, HEAD_DIM))
    n_front = 7 * WIDTH

    def cast(w):
        return w.astype(MXU_DTYPE)

    def vec(w):
        return w.reshape(1, -1).astype(F32)

    k_outs, v_outs, st_p, st_s = [], [], [], []
    for l in range(depth):
        h = _ffn_half(h, vec(ffn1_pre_norm[l]), vec(ffn1_post_norm[l]), cast(ffn1_w_gate[l]),
                      cast(ffn1_w_up[l]), cast(ffn1_w_down[l]), tm)
        w_l = cast(w_in[l])
        qa, ka, va, lf, zg, qb, kb, vb, k_out, v_out = _mixer_front(
            h, vec(mix_pre_norm[l]), w_l[:, :n_front], vec(lbs[l]), tm)
        oa_p, s_p = _gla_prompt(qa, ka, va, lf, batch, seq_rows)
        oa_s, s_s = _gla_sample(qa, ka, va, lf, state_hgrn, l, n_prompt,
                                dec_batch, steps)
        ob_p, ob_s = _attention(page_table, attn_scalars[l], qb, kb, vb, cache_k, cache_v, l,
                                batch, seq_rows, dec_batch, steps)
        h = _mixer_back(h, jnp.concatenate([oa_p, oa_s], axis=0), zg,
                        jnp.concatenate([ob_p, ob_s], axis=0), vec(mix_pre_norm[l]),
                        vec(mix_post_norm[l]), vec(hgrn_norm_w[l]), w_l[:, n_front:],
                        cast(w_branch_a[l]), cast(w_branch_b[l]), cast(w_out[l]), tm)
        h = _ffn_half(h, vec(ffn2_pre_norm[l]), vec(ffn2_post_norm[l]), cast(ffn2_w_gate[l]),
                      cast(ffn2_w_up[l]), cast(ffn2_w_down[l]), tm)
        k_outs.append(k_out)
        v_outs.append(v_out)
        st_p.append(s_p.astype(state_hgrn.dtype))
        st_s.append(s_s.astype(state_hgrn.dtype))
    k_p, k_s, v_p, v_s = _collect(k_outs, v_outs, n_prompt * N_HEADS)
    prompt_shape = (depth, batch, seq_rows, N_HEADS, HEAD_DIM)
    sample_shape = (depth, dec_batch, steps, N_HEADS, HEAD_DIM)
    y_prompt = h[:n_prompt].reshape(batch, seq_rows, d)[:, N_META:]
    y_sample = h[n_prompt:].reshape(dec_batch, steps, d)
    return (y_prompt, y_sample, k_p.reshape(prompt_shape), v_p.reshape(prompt_shape),
            jnp.stack(st_p), k_s.reshape(sample_shape), v_s.reshape(sample_shape),
            jnp.stack(st_s))
```

```python
import functools
import math

import jax
import jax.numpy as jnp
from jax import lax
from jax.experimental import pallas as pl
from jax.experimental.pallas import tpu as pltpu

F32 = jnp.float32
MXU_DTYPE = jnp.bfloat16
RMS_EPS = 1e-6
N_META = 16
N_HEADS = 4
HEAD_DIM = 128
HALF_DIM = HEAD_DIM // 2
WIDTH = N_HEADS * HEAD_DIM
PAGE = 128
GLA_CHUNK = 64
GLA_UNROLL = 4
GLA_SAFE_DECAY = 80.0
ATT_TILE = 256
ATT_TAIL = 128
ONES_ROWS = 16
LOG2_E = 1.4426950408889634
ROWS_PER_STEP = 16
FF_CHUNK = 1024
ROW_TILE_TARGET = 896
NEG_BIG = -1e30
VMEM_LIMIT = 56 * 1024 * 1024


def _rms(x):
    return x * lax.rsqrt(jnp.mean(x * x, axis=-1, keepdims=True) + RMS_EPS)


def _dot(a, b):
    return jnp.dot(a, b, preferred_element_type=F32)


def _dot_nt(a, b):
    return lax.dot_general(a, b, (((1,), (1,)), ((), ())), preferred_element_type=F32)


def _dot_tn(a, b):
    return lax.dot_general(a, b, (((0,), (0,)), ((), ())), preferred_element_type=F32)


def _silu(x):
    return x * jax.nn.sigmoid(x)


def _row_tile(n, target, align=16):
    best = None
    for t in range(align, min(n, target) + 1, align):
        if n % t == 0:
            best = t
    assert best is not None, (n, target)
    return best


def _chunks(total, size):
    return tuple((c, min(c + size, total)) for c in range(0, total, size))


def _params(*sem):
    return pltpu.CompilerParams(dimension_semantics=sem, vmem_limit_bytes=VMEM_LIMIT)


def _const_spec(shape):
    return pl.BlockSpec(shape, lambda *_: (0,) * len(shape), pipeline_mode=pl.Buffered(1))


def _ffn_kernel(h_ref, pre_ref, post_ref, wg_ref, wu_ref, wd_ref, o_ref, *, chunks):
    x = h_ref[...]
    xn = (_rms(x) * pre_ref[...]).astype(MXU_DTYPE)
    y = None
    for c0, c1 in chunks:
        g = _dot(xn, wg_ref[:, c0:c1])
        u = _dot(xn, wu_ref[:, c0:c1])
        part = _dot((_silu(g) * u).astype(MXU_DTYPE), wd_ref[c0:c1, :])
        y = part if y is None else y + part
    o_ref[...] = x + 0.5 * (_rms(y) * post_ref[...])


def _ffn_half(h, pre, post, wg, wu, wd, tm):
    n, d = h.shape
    f = wg.shape[1]
    row = pl.BlockSpec((tm, d), lambda i: (i, 0))
    return pl.pallas_call(
        functools.partial(_ffn_kernel, chunks=_chunks(f, FF_CHUNK)),
        grid=(n // tm,),
        in_specs=[row, _const_spec((1, d)), _const_spec((1, d)),
                  _const_spec((d, f)), _const_spec((d, f)), _const_spec((f, d))],
        out_specs=row,
        out_shape=jax.ShapeDtypeStruct((n, d), F32),
        compiler_params=_params("parallel"),
        name="ffn_half",
    )(h, pre, post, wg, wu, wd)


def _front_kernel(h_ref, pre_ref, w_ref, lb_ref,
                  qa_ref, ka_ref, va_ref, lf_ref, zg_ref, qb_ref, kb_ref, vb_ref,
                  kout_ref, vout_ref):
    x = h_ref[...]
    xn = (_rms(x) * pre_ref[...]).astype(MXU_DTYPE)

    def proj(i):
        return _dot(xn, w_ref[:, i * WIDTH:(i + 1) * WIDTH])

    qa_ref[...] = proj(0).astype(qa_ref.dtype)
    zf = proj(1)
    lb = lb_ref[...]
    e = jnp.exp(-jnp.abs(zf))
    r = 1.0 / (1.0 + e)
    sig_pos = jnp.where(zf >= 0, r, e * r)
    sig_neg = jnp.where(zf >= 0, e * r, r)
    lf_ref[...] = jnp.log(lb + (1.0 - lb) * sig_pos)
    ka_ref[...] = ((1.0 - lb) * sig_neg).astype(ka_ref.dtype)
    va_ref[...] = proj(2).astype(va_ref.dtype)
    zg_ref[...] = proj(3).astype(zg_ref.dtype)
    qb_ref[...] = proj(4).astype(qb_ref.dtype)
    tm = x.shape[0]
    for src, dst_ref, out_ref in ((proj(5), kb_ref, kout_ref), (proj(6), vb_ref, vout_ref)):
        dst_ref[...] = src.astype(dst_ref.dtype)
        for hd in range(N_HEADS):
            out_ref[pl.ds(hd, tm, stride=N_HEADS), :] = src[:, hd * HEAD_DIM:(hd + 1) * HEAD_DIM]


def _mixer_front(h, pre, w_front, lb, tm):
    n, d = h.shape
    row = pl.BlockSpec((tm, d), lambda i: (i, 0))
    col = pl.BlockSpec((tm, WIDTH), lambda i: (i, 0))
    half = jax.ShapeDtypeStruct((n, WIDTH), MXU_DTYPE)
    full = jax.ShapeDtypeStruct((n, WIDTH), F32)
    per_head = jax.ShapeDtypeStruct((n * N_HEADS, HEAD_DIM), F32)
    per_head_blk = pl.BlockSpec((tm * N_HEADS, HEAD_DIM), lambda i: (i, 0))
    return pl.pallas_call(
        _front_kernel,
        grid=(n // tm,),
        in_specs=[row, _const_spec((1, d)), _const_spec((d, 7 * WIDTH)),
                  _const_spec((1, WIDTH))],
        out_specs=[col] * 8 + [per_head_blk] * 2,
        out_shape=[half, half, half, full, half, half, half, half, per_head, per_head],
        compiler_params=_params("parallel"),
        name="mixer_front",
    )(h, pre, w_front, lb)


def _back_kernel(h_ref, oa_ref, zg_ref, ob_ref, pre_ref, post_ref, nw_ref,
                 wgate_ref, wa_ref, wb_ref, wo_ref, o_ref):
    x = h_ref[...]
    d = x.shape[1]
    xn = (_rms(x) * pre_ref[...]).astype(MXU_DTYPE)
    oa = oa_ref[...].astype(F32)
    heads = [_rms(oa[:, h * HEAD_DIM:(h + 1) * HEAD_DIM]) * nw_ref[...]
             for h in range(N_HEADS)]
    ya = (jnp.concatenate(heads, axis=1) * _silu(zg_ref[...].astype(F32))).astype(MXU_DTYPE)
    gate_a = jax.nn.sigmoid(_dot(xn, wgate_ref[:, :d]))
    gate_b = jax.nn.sigmoid(_dot(xn, wgate_ref[:, d:]))
    merged = gate_a * _dot(ya, wa_ref[...]) + gate_b * _dot(ob_ref[...], wb_ref[...])
    out = _dot(merged.astype(MXU_DTYPE), wo_ref[...])
    o_ref[...] = x + _rms(out) * post_ref[...]


def _mixer_back(h, oa, zg, ob, pre, post, nw, w_gate, wa, wb, wo, tm):
    n, d = h.shape
    row = pl.BlockSpec((tm, d), lambda i: (i, 0))
    col = pl.BlockSpec((tm, WIDTH), lambda i: (i, 0))
    return pl.pallas_call(
        _back_kernel,
        grid=(n // tm,),
        in_specs=[row, col, col, col, _const_spec((1, d)), _const_spec((1, d)),
                  _const_spec((1, HEAD_DIM)), _const_spec((d, 2 * d)),
                  _const_spec((WIDTH, d)), _const_spec((WIDTH, d)), _const_spec((d, d))],
        out_specs=row,
        out_shape=jax.ShapeDtypeStruct((n, d), F32),
        compiler_params=_params("parallel"),
        name="mixer_back",
    )(h, oa, zg, ob, pre, post, nw, w_gate, wa, wb, wo)


def _cumsum_rows(x):
    rows = x.shape[0]
    row = lax.broadcasted_iota(jnp.int32, x.shape, 0)
    shift = 1
    while shift < rows:
        x = x + jnp.where(row >= shift, pltpu.roll(x, shift, 0), 0.0)
        shift *= 2
    return x


def _scores_direct(q, k, cum):
    c = q.shape[0]
    col = lax.broadcasted_iota(jnp.int32, (c, c), 1)
    row = lax.broadcasted_iota(jnp.int32, (c, 1), 0)

    def body(s, acc):
        pick = row == s
        ks = jnp.sum(jnp.where(pick, k, 0.0), axis=0, keepdims=True)
        cs = jnp.sum(jnp.where(pick, cum, 0.0), axis=0, keepdims=True)
        t = q * ks * jnp.exp(jnp.minimum(cum - cs, 0.0))
        return jnp.where(col == s, jnp.sum(t, axis=1, keepdims=True), acc)

    return lax.fori_loop(0, c, body, jnp.zeros((c, c), F32))


def _gla_chunk(q, k, v, g, state, direct):
    c = q.shape[0]
    cum = _cumsum_rows(g)
    last = cum[c - 1:c, :]
    q_dec = (q * jnp.exp(cum)).astype(MXU_DTYPE)
    k_end = (k * jnp.exp(last - cum)).astype(MXU_DTYPE)
    v_mx = v.astype(MXU_DTYPE)
    state_decay = jnp.exp(last)
    if not direct:
        k_inv = (k * jnp.exp(-cum)).astype(MXU_DTYPE)
    tril = (lax.broadcasted_iota(jnp.int32, (c, c), 1)
            <= lax.broadcasted_iota(jnp.int32, (c, c), 0))
    outs, new_state = [], []
    for h in range(N_HEADS):
        sl = slice(h * HEAD_DIM, (h + 1) * HEAD_DIM)
        if direct:
            scores = _scores_direct(q[:, sl], k[:, sl], cum[:, sl])
        else:
            scores = _dot_nt(q_dec[:, sl], k_inv[:, sl])
        scores = jnp.where(tril, scores, 0.0).astype(MXU_DTYPE)
        outs.append(_dot_nt(q_dec[:, sl], state[h].astype(MXU_DTYPE)) + _dot(scores, v_mx[:, sl]))
        new_state.append(state[h] * state_decay[:, sl] + _dot_tn(v_mx[:, sl], k_end[:, sl]))
    return jnp.concatenate(outs, axis=1), new_state


def _gla_prompt_kernel(q_ref, k_ref, v_ref, g_ref, o_ref, s_ref):
    rows = q_ref.shape[0]
    n_chunks = (rows - N_META) // GLA_CHUNK

    def run(r0, c, state, direct):
        sl = pl.ds(r0, c)
        o, state = _gla_chunk(q_ref[sl, :].astype(F32), k_ref[sl, :].astype(F32),
                              v_ref[sl, :].astype(F32), g_ref[sl, :], state, direct)
        o_ref[sl, :] = o.astype(o_ref.dtype)
        return state

    def run_all(direct):
        state = [jnp.zeros((HEAD_DIM, HEAD_DIM), F32)] * N_HEADS
        state = run(0, N_META, state, direct)

        def body(i, state):
            state = list(state)
            for u in range(GLA_UNROLL):
                r0 = pl.multiple_of(N_META + (i * GLA_UNROLL + u) * GLA_CHUNK, 16)
                state = run(r0, GLA_CHUNK, state, direct)
            return tuple(state)

        state = lax.fori_loop(0, n_chunks // GLA_UNROLL, body, tuple(state))
        for h in range(N_HEADS):
            s_ref[0, h] = state[h].T

    g_main = g_ref[N_META:rows, :].reshape(n_chunks, GLA_CHUNK, WIDTH)
    decay = jnp.maximum(jnp.max(-jnp.sum(g_main, axis=1)),
                        jnp.max(-jnp.sum(g_ref[0:N_META, :], axis=0)))
    safe = decay <= GLA_SAFE_DECAY
    pl.when(safe)(functools.partial(run_all, False))
    pl.when(jnp.logical_not(safe))(functools.partial(run_all, True))


def _gla_prompt(qa, ka, va, lf, batch, seq_rows):
    blk = pl.BlockSpec((seq_rows, WIDTH), lambda b: (b, 0))
    return pl.pallas_call(
        _gla_prompt_kernel,
        grid=(batch,),
        in_specs=[blk] * 4,
        out_specs=[blk, pl.BlockSpec((1, N_HEADS, HEAD_DIM, HEAD_DIM), lambda b: (b, 0, 0, 0))],
        out_shape=[jax.ShapeDtypeStruct((batch * seq_rows, WIDTH), MXU_DTYPE),
                   jax.ShapeDtypeStruct((batch, N_HEADS, HEAD_DIM, HEAD_DIM), F32)],
        compiler_params=_params("parallel"),
        name="gla_prompt",
    )(qa, ka, va, lf)


def _gla_sample_kernel(q_ref, k_ref, v_ref, g_ref, s0_ref, o_ref, s_ref, *, steps):
    q = q_ref[...].astype(F32)
    k = k_ref[...].astype(F32)
    v = v_ref[...].astype(F32)
    g = g_ref[...]
    row = lax.broadcasted_iota(jnp.int32, q.shape, 0)

    def run_all(direct):
        out = jnp.zeros(q.shape, F32)
        for bi in range(ROWS_PER_STEP // steps):
            mine = (row >= bi * steps) & (row < (bi + 1) * steps)
            state = [s0_ref[bi, h].T for h in range(N_HEADS)]
            o, state = _gla_chunk(q, jnp.where(mine, k, 0.0), v, jnp.where(mine, g, 0.0),
                                  state, direct)
            out = jnp.where(mine, o, out)
            for h in range(N_HEADS):
                s_ref[bi, h] = state[h].T
        o_ref[...] = out.astype(o_ref.dtype)

    safe = jnp.max(-jnp.sum(g, axis=0)) <= GLA_SAFE_DECAY
    pl.when(safe)(functools.partial(run_all, False))
    pl.when(jnp.logical_not(safe))(functools.partial(run_all, True))


def _gla_sample(qa, ka, va, lf, states, layer, row0, dec_batch, steps):
    per = ROWS_PER_STEP // steps
    blk = pl.BlockSpec((ROWS_PER_STEP, WIDTH), lambda i: (row0 // ROWS_PER_STEP + i, 0))
    sblk = pl.BlockSpec((per, N_HEADS, HEAD_DIM, HEAD_DIM), lambda i: (i, 0, 0, 0))
    s0blk = pl.BlockSpec((None, per, N_HEADS, HEAD_DIM, HEAD_DIM), lambda i: (layer, i, 0, 0, 0))
    return pl.pallas_call(
        functools.partial(_gla_sample_kernel, steps=steps),
        grid=(dec_batch // per,),
        in_specs=[blk] * 4 + [s0blk],
        out_specs=[pl.BlockSpec((ROWS_PER_STEP, WIDTH), lambda i: (i, 0)), sblk],
        out_shape=[jax.ShapeDtypeStruct((dec_batch * steps, WIDTH), MXU_DTYPE),
                   jax.ShapeDtypeStruct(states.shape[1:], F32)],
        compiler_params=_params("parallel"),
        name="gla_sample",
    )(qa, ka, va, lf, states)


def _diff_finish(o1, o2, lam, scale, axis):
    d = o1 - lam * o2
    return d * lax.rsqrt(jnp.mean(d * d, axis=axis, keepdims=True) + RMS_EPS) * scale


def _attn_kernel(pt_ref, sc_ref, q_ref, k_ref, v_ref, qs_ref, kn_ref, vn_ref, ck_ref, cv_ref,
                 o_ref, os_ref, kbf_ref, vt_ref, s_ref, p_ref, kbuf, vbuf, sem,
                 *, layer, n_pages, dec_batch, steps):
    rows = q_ref.shape[0]
    n_full = rows // ATT_TILE
    rem = rows - n_full * ATT_TILE
    lam = sc_ref[0]
    out_scale = sc_ref[1]
    step_id = pl.program_id(0) * pl.num_programs(1) + pl.program_id(1)
    per = ROWS_PER_STEP // steps

    def copies(b, slot):
        out = []
        for p in range(n_pages):
            page = pt_ref[b * n_pages + p]
            dst = pl.ds(p * PAGE * N_HEADS, PAGE * N_HEADS)
            out.append(pltpu.make_async_copy(ck_ref.at[layer, page], kbuf.at[slot, dst],
                                             sem.at[slot, 0]))
            out.append(pltpu.make_async_copy(cv_ref.at[layer, page], vbuf.at[slot, dst],
                                             sem.at[slot, 1]))
        return out

    @pl.when(step_id == 0)
    def _():
        for c in copies(0, 0):
            c.start()

    kbf_ref[0:rows, :] = k_ref[...].astype(MXU_DTYPE)
    kbf_ref[rows:, :] = jnp.zeros((kbf_ref.shape[0] - rows, HEAD_DIM), MXU_DTYPE)
    for i in range(n_full):
        vt_ref[0:HEAD_DIM, i * ATT_TILE:(i + 1) * ATT_TILE] = (
            v_ref[i * ATT_TILE:(i + 1) * ATT_TILE, :].astype(F32).T.astype(MXU_DTYPE))
    v_tail = jnp.concatenate(
        [v_ref[n_full * ATT_TILE:rows, :].astype(F32),
         jnp.zeros((ATT_TILE - rem, HEAD_DIM), F32)], axis=0)
    vt_ref[0:HEAD_DIM, n_full * ATT_TILE:] = v_tail.T.astype(MXU_DTYPE)
    vt_ref[HEAD_DIM:, :] = jnp.ones((ONES_ROWS, vt_ref.shape[1]), MXU_DTYPE)

    def tile_scores(q0, tq, out_rows, slot):
        q_t = q_ref[q0:q0 + tq, :].astype(F32).T * (HALF_DIM ** -0.5 * LOG2_E)
        comp = lax.broadcasted_iota(jnp.int32, q_t.shape, 0) < HALF_DIM
        q_both = jnp.concatenate([jnp.where(comp, q_t, 0.0), jnp.where(comp, 0.0, q_t)],
                                 axis=1).astype(MXU_DTYPE)
        w = 2 * tq
        visible = q0 // ATT_TILE
        tiles = [(i * ATT_TILE, ATT_TILE, i >= visible)
                 for i in range(n_full) if i * ATT_TILE < q0 + tq]
        if n_full * ATT_TILE < q0 + tq:
            tiles.append((n_full * ATT_TILE, ATT_TAIL, True))
        k_tot = tiles[-1][0] + tiles[-1][1]

        m = None
        for kv0, tk, masked in tiles:
            s = _dot(kbf_ref[kv0:kv0 + tk, :], q_both)
            if masked:
                kpos = kv0 + lax.broadcasted_iota(jnp.int32, s.shape, 0)
                qcol = lax.broadcasted_iota(jnp.int32, s.shape, 1)
                qpos = q0 + jnp.where(qcol >= tq, qcol - tq, qcol)
                s = jnp.where(kpos <= qpos, s, NEG_BIG)
            s_ref[slot, kv0:kv0 + tk, 0:w] = s
            tile_max = jnp.max(s, axis=0, keepdims=True)
            m = tile_max if m is None else jnp.maximum(m, tile_max)
        return tiles, m

    def tile_finish(q0, tq, out_rows, slot, tiles, m):
        w = 2 * tq
        k_tot = tiles[-1][0] + tiles[-1][1]
        for kv0, tk, _ in tiles:
            p_ref[slot, kv0:kv0 + tk, 0:w] = jnp.exp2(
                s_ref[slot, kv0:kv0 + tk, 0:w] - m).astype(MXU_DTYPE)
        acc = _dot(vt_ref[:, 0:k_tot], p_ref[slot, 0:k_tot, 0:w])
        o = acc[:HEAD_DIM] / acc[HEAD_DIM:HEAD_DIM + 1]
        d = _diff_finish(o[:, :tq], o[:, tq:], lam, out_scale, 0)
        o_ref[q0 + tq - out_rows:q0 + tq, :] = d.T[tq - out_rows:, :].astype(o_ref.dtype)

    def heads_to_lanes(buf, slot):
        past = n_pages * PAGE
        return jnp.concatenate(
            [buf[slot, pl.ds(h, past, stride=N_HEADS), :].astype(MXU_DTYPE)
             for h in range(N_HEADS)], axis=1)

    q_all = qs_ref[...].astype(F32) * (HALF_DIM ** -0.5)
    k_new = kn_ref[...].astype(MXU_DTYPE)
    v_new = vn_ref[...].astype(MXU_DTYPE)
    n_q = 2 * N_HEADS * 8
    g_row = lax.broadcasted_iota(jnp.int32, (n_q, WIDTH), 0) // 8
    g_lane = lax.broadcasted_iota(jnp.int32, (n_q, WIDTH), 1) // HALF_DIM
    new_row = lax.broadcasted_iota(jnp.int32, (n_q, ROWS_PER_STEP), 0) % 8
    new_col = lax.broadcasted_iota(jnp.int32, (n_q, ROWS_PER_STEP), 1)

    def decode(bi):
        b = step_id * per + bi
        slot = bi % 2

        @pl.when(b + 1 < dec_batch)
        def _():
            for c in copies(b + 1, 1 - slot):
                c.start()

        for c in copies(b, slot):
            c.wait()

        q8 = q_all[8 * (bi * steps // 8):8 * (bi * steps // 8) + 8, :]
        q_bd = jnp.where(g_row == g_lane, jnp.concatenate([q8] * (2 * N_HEADS), axis=0),
                         0.0).astype(MXU_DTYPE)
        s_past = _dot_nt(q_bd, heads_to_lanes(kbuf, slot))
        s_new = _dot_nt(q_bd, k_new)
        new_ok = (new_col // steps == bi) & (new_col % steps <= new_row % steps)
        s_new = jnp.where(new_ok, s_new, NEG_BIG)
        m = jnp.maximum(jnp.max(s_past, axis=1, keepdims=True),
                        jnp.max(s_new, axis=1, keepdims=True))
        p_past = jnp.exp(s_past - m)
        p_new = jnp.exp(s_new - m)
        l = jnp.sum(p_past, axis=1, keepdims=True) + jnp.sum(p_new, axis=1, keepdims=True)
        o = (_dot(p_past.astype(MXU_DTYPE), heads_to_lanes(vbuf, slot))
             + _dot(p_new.astype(MXU_DTYPE), v_new)) / l
        heads = []
        for h in range(N_HEADS):
            sl = slice(h * HEAD_DIM, (h + 1) * HEAD_DIM)
            heads.append(_diff_finish(o[16 * h:16 * h + 8, sl], o[16 * h + 8:16 * h + 16, sl],
                                      lam, out_scale, 1))
        return jnp.concatenate(heads, axis=1)

    q_tiles = [(j * ATT_TILE, ATT_TILE, ATT_TILE) for j in range(n_full)]
    if rem:
        q_tiles.append((rows - ATT_TAIL, ATT_TAIL, rem))
    share = -(-len(q_tiles) // per)
    results = [decode(0)]
    pending = tile_scores(*q_tiles[0], 0)
    for j in range(len(q_tiles)):
        ahead = tile_scores(*q_tiles[j + 1], (j + 1) % 2) if j + 1 < len(q_tiles) else None
        tile_finish(*q_tiles[j], j % 2, *pending)
        pending = ahead
        if (j + 1) % share == 0 and len(results) < per:
            results.append(decode(len(results)))
    while len(results) < per:
        results.append(decode(len(results)))

    row8 = lax.broadcasted_iota(jnp.int32, (8, WIDTH), 0)
    merged = []
    for pair in range(ROWS_PER_STEP // 8):
        acc = results[pair * (8 // steps)]
        for j in range(1, 8 // steps):
            acc = jnp.where(row8 // steps == j, results[pair * (8 // steps) + j], acc)
        merged.append(acc)
    os_ref[...] = jnp.concatenate(merged, axis=0).astype(os_ref.dtype)


def _attention(page_table, scalars, qb, kb, vb, cache_k, cache_v, layer, batch, seq_rows,
               dec_batch, steps):
    assert seq_rows >= ATT_TILE and seq_rows % ATT_TILE <= ATT_TAIL
    assert 8 % steps == 0 and dec_batch * steps == batch * N_HEADS * ROWS_PER_STEP
    n_full = seq_rows // ATT_TILE
    kv_rows = n_full * ATT_TILE + ATT_TAIL
    n_pages = page_table.shape[1]
    sample0 = batch * seq_rows // ROWS_PER_STEP
    blk = pl.BlockSpec((seq_rows, HEAD_DIM), lambda b, h, pt: (b, h))
    sblk = pl.BlockSpec((ROWS_PER_STEP, WIDTH), lambda b, h, pt: (sample0 + b * N_HEADS + h, 0))
    grid_spec = pltpu.PrefetchScalarGridSpec(
        num_scalar_prefetch=1,
        grid=(batch, N_HEADS),
        in_specs=[pl.BlockSpec(memory_space=pltpu.SMEM), blk, blk, blk, sblk, sblk, sblk,
                  pl.BlockSpec(memory_space=pl.ANY), pl.BlockSpec(memory_space=pl.ANY)],
        out_specs=[blk, pl.BlockSpec((ROWS_PER_STEP, WIDTH), lambda b, h, pt: (b * N_HEADS + h, 0))],
        scratch_shapes=[pltpu.VMEM((kv_rows, HEAD_DIM), MXU_DTYPE),
                        pltpu.VMEM((HEAD_DIM + ONES_ROWS, (n_full + 1) * ATT_TILE), MXU_DTYPE),
                        pltpu.VMEM((2, kv_rows, 2 * ATT_TILE), F32),
                        pltpu.VMEM((2, kv_rows, 2 * ATT_TILE), MXU_DTYPE),
                        pltpu.VMEM((2, n_pages * PAGE * N_HEADS, HEAD_DIM), F32),
                        pltpu.VMEM((2, n_pages * PAGE * N_HEADS, HEAD_DIM), F32),
                        pltpu.SemaphoreType.DMA((2, 2))],
    )
    return pl.pallas_call(
        functools.partial(_attn_kernel, layer=layer, n_pages=n_pages, dec_batch=dec_batch,
                          steps=steps),
        grid_spec=grid_spec,
        out_shape=[jax.ShapeDtypeStruct((batch * seq_rows, WIDTH), MXU_DTYPE),
                   jax.ShapeDtypeStruct((dec_batch * steps, WIDTH), MXU_DTYPE)],
        compiler_params=_params("arbitrary", "arbitrary"),
        name="attention",
    )(page_table.reshape(-1), scalars, qb, kb, vb, qb, kb, vb, cache_k, cache_v)


def _collect_kernel(*refs, depth):
    k_srcs, v_srcs = refs[:depth], refs[depth:2 * depth]
    k_dst, v_dst = refs[2 * depth:]
    layer = pl.program_id(0)
    for l in range(depth):
        @pl.when(layer == l)
        def _():
            k_dst[...] = k_srcs[l][...]
            v_dst[...] = v_srcs[l][...]


def _collect(k_outs, v_outs, rows, block_rows):
    depth = len(k_outs)
    assert rows % block_rows == 0 and block_rows % 8 == 0
    n_blocks = rows // block_rows

    def src_spec(l):
        return pl.BlockSpec(
            (block_rows, HEAD_DIM),
            lambda d, i: (jnp.where(d == l, i, jnp.where(d < l, 0, n_blocks - 1)), 0))

    dst_spec = pl.BlockSpec((None, block_rows, HEAD_DIM), lambda d, i: (d, i, 0))
    stacked = jax.ShapeDtypeStruct((depth, rows, HEAD_DIM), F32)
    return pl.pallas_call(
        functools.partial(_collect_kernel, depth=depth),
        grid=(depth, n_blocks),
        in_specs=[src_spec(l) for l in range(depth)] * 2,
        out_specs=[dst_spec, dst_spec],
        out_shape=[stacked, stacked],
        compiler_params=_params("arbitrary", "arbitrary"),
        name="collect_kv",
    )(*k_outs, *v_outs)


def kernel(x_prompt, x_sample, cache_k, cache_v, state_hgrn, page_table, meta_tokens, ffn1_pre_norm, ffn1_post_norm, ffn1_w_gate, ffn1_w_up, ffn1_w_down, mix_pre_norm, mix_post_norm, w_in, hgrn_lb_logits, hgrn_norm_w, lambda_q1, lambda_k1, lambda_q2, lambda_k2, w_branch_a, w_branch_b, w_out, ffn2_pre_norm, ffn2_post_norm, ffn2_w_gate, ffn2_w_up, ffn2_w_down):
    batch, seq, d = x_prompt.shape
    dec_batch, steps = x_sample.shape[:2]
    depth = w_in.shape[0]
    seq_rows = N_META + seq
    n_prompt = batch * seq_rows
    n_sample = dec_batch * steps
    n = n_prompt + n_sample
    assert n_prompt % ROWS_PER_STEP == 0 and ROWS_PER_STEP % steps == 0
    assert (seq_rows - N_META) % (GLA_CHUNK * GLA_UNROLL) == 0
    tm = _row_tile(n, ROW_TILE_TARGET)

    lbs = jnp.cumsum(jax.nn.softmax(hgrn_lb_logits.astype(F32), axis=0), axis=0)
    lbs = lbs - lbs[0:1]
    lam_init = jnp.asarray([0.8 - 0.6 * math.exp(-0.3 * l) for l in range(depth)], F32)
    lam = (jnp.exp(jnp.sum(lambda_q1.astype(F32) * lambda_k1.astype(F32), axis=-1))
           - jnp.exp(jnp.sum(lambda_q2.astype(F32) * lambda_k2.astype(F32), axis=-1)) + lam_init)
    attn_scalars = jnp.stack([lam, 1.0 - lam_init], axis=1)

    meta = jnp.broadcast_to(meta_tokens[None].astype(x_prompt.dtype), (batch, N_META, d))
    h = jnp.concatenate([jnp.concatenate([meta, x_prompt], axis=1).reshape(n_prompt, d),
                         x_sample.reshape(n_sample, d)], axis=0)
    cache_k = cache_k.reshape(cache_k.shape[:2] + (PAGE * N_HEADS, HEAD_DIM))
    cache_v = cache_v.reshape(cache_v.shape[:2] + (PAGE * N_HEADS, HEAD_DIM))
    n_front = 7 * WIDTH

    def cast(w):
        return w.astype(MXU_DTYPE)

    def vec(w):
        return w.reshape(1, -1).astype(F32)

    k_outs, v_outs, st_p, st_s = [], [], [], []
    for l in range(depth):
        h = _ffn_half(h, vec(ffn1_pre_norm[l]), vec(ffn1_post_norm[l]), cast(ffn1_w_gate[l]),
                      cast(ffn1_w_up[l]), cast(ffn1_w_down[l]), tm)
        w_l = cast(w_in[l])
        qa, ka, va, lf, zg, qb, kb, vb, k_out, v_out = _mixer_front(
            h, vec(mix_pre_norm[l]), w_l[:, :n_front], vec(lbs[l]), tm)
        oa_p, s_p = _gla_prompt(qa, ka, va, lf, batch, seq_rows)
        oa_s, s_s = _gla_sample(qa, ka, va, lf, state_hgrn, l, n_prompt,
                                dec_batch, steps)
        ob_p, ob_s = _attention(page_table, attn_scalars[l], qb, kb, vb, cache_k, cache_v, l,
                                batch, seq_rows, dec_batch, steps)
        h = _mixer_back(h, jnp.concatenate([oa_p, oa_s], axis=0), zg,
                        jnp.concatenate([ob_p, ob_s], axis=0), vec(mix_pre_norm[l]),
                        vec(mix_post_norm[l]), vec(hgrn_norm_w[l]), w_l[:, n_front:],
                        cast(w_branch_a[l]), cast(w_branch_b[l]), cast(w_out[l]), tm)
        h = _ffn_half(h, vec(ffn2_pre_norm[l]), vec(ffn2_post_norm[l]), cast(ffn2_w_gate[l]),
                      cast(ffn2_w_up[l]), cast(ffn2_w_down[l]), tm)
        k_outs.append(k_out)
        v_outs.append(v_out)
        st_p.append(s_p.astype(state_hgrn.dtype))
        st_s.append(s_s.astype(state_hgrn.dtype))
    split = n_prompt * N_HEADS
    k_p, v_p = _collect(k_outs, v_outs, split, seq_rows)
    k_s = jnp.stack([k[split:] for k in k_outs])
    v_s = jnp.stack([v[split:] for v in v_outs])
    prompt_shape = (depth, batch, seq_rows, N_HEADS, HEAD_DIM)
    sample_shape = (depth, dec_batch, steps, N_HEADS, HEAD_DIM)
    y_prompt = h[:n_prompt].reshape(batch, seq_rows, d)[:, N_META:]
    y_sample = h[n_prompt:].reshape(dec_batch, steps, d)
    return (y_prompt, y_sample, k_p.reshape(prompt_shape), v_p.reshape(prompt_shape),
            jnp.stack(st_p), k_s.reshape(sample_shape), v_s.reshape(sample_shape),
            jnp.stack(st_s))
```

```python
import functools
import math

import jax
import jax.numpy as jnp
from jax import lax
from jax.experimental import pallas as pl
from jax.experimental.pallas import tpu as pltpu

F32 = jnp.float32
MXU_DTYPE = jnp.bfloat16
RMS_EPS = 1e-6
N_META = 16
N_HEADS = 4
HEAD_DIM = 128
HALF_DIM = HEAD_DIM // 2
WIDTH = N_HEADS * HEAD_DIM
PAGE = 128
GLA_CHUNK = 64
GLA_UNROLL = 4
GLA_SAFE_DECAY = 80.0
ATT_TILE = 256
ATT_TAIL = 128
ONES_ROWS = 16
LOG2_E = 1.4426950408889634
ROWS_PER_STEP = 16
FF_CHUNK = 1024
ROW_TILE_TARGET = 896
NEG_BIG = -1e30
VMEM_LIMIT = 56 * 1024 * 1024


def _rms(x):
    return x * lax.rsqrt(jnp.mean(x * x, axis=-1, keepdims=True) + RMS_EPS)


def _dot(a, b):
    return jnp.dot(a, b, preferred_element_type=F32)


def _dot_nt(a, b):
    return lax.dot_general(a, b, (((1,), (1,)), ((), ())), preferred_element_type=F32)


def _dot_tn(a, b):
    return lax.dot_general(a, b, (((0,), (0,)), ((), ())), preferred_element_type=F32)


def _silu(x):
    return x * jax.nn.sigmoid(x)


def _row_tile(n, target, align=16):
    best = None
    for t in range(align, min(n, target) + 1, align):
        if n % t == 0:
            best = t
    assert best is not None, (n, target)
    return best


def _chunks(total, size):
    return tuple((c, min(c + size, total)) for c in range(0, total, size))


def _params(*sem):
    return pltpu.CompilerParams(dimension_semantics=sem, vmem_limit_bytes=VMEM_LIMIT)


def _const_spec(shape):
    return pl.BlockSpec(shape, lambda *_: (0,) * len(shape), pipeline_mode=pl.Buffered(1))


def _ffn_kernel(h_ref, pre_ref, post_ref, wg_ref, wu_ref, wd_ref, o_ref, *, chunks):
    x = h_ref[...]
    xn = (_rms(x) * pre_ref[...]).astype(MXU_DTYPE)
    y = None
    for c0, c1 in chunks:
        g = _dot(xn, wg_ref[:, c0:c1])
        u = _dot(xn, wu_ref[:, c0:c1])
        part = _dot((_silu(g) * u).astype(MXU_DTYPE), wd_ref[c0:c1, :])
        y = part if y is None else y + part
    o_ref[...] = x + 0.5 * (_rms(y) * post_ref[...])


def _ffn_half(h, pre, post, wg, wu, wd, tm):
    n, d = h.shape
    f = wg.shape[1]
    row = pl.BlockSpec((tm, d), lambda i: (i, 0))
    return pl.pallas_call(
        functools.partial(_ffn_kernel, chunks=_chunks(f, FF_CHUNK)),
        grid=(n // tm,),
        in_specs=[row, _const_spec((1, d)), _const_spec((1, d)),
                  _const_spec((d, f)), _const_spec((d, f)), _const_spec((f, d))],
        out_specs=row,
        out_shape=jax.ShapeDtypeStruct((n, d), F32),
        compiler_params=_params("parallel"),
        name="ffn_half",
    )(h, pre, post, wg, wu, wd)


def _front_kernel(h_ref, pre_ref, w_ref, lb_ref,
                  qa_ref, ka_ref, va_ref, lf_ref, zg_ref, qb_ref, kb_ref, vb_ref,
                  kout_ref, vout_ref):
    x = h_ref[...]
    xn = (_rms(x) * pre_ref[...]).astype(MXU_DTYPE)

    def proj(i):
        return _dot(xn, w_ref[:, i * WIDTH:(i + 1) * WIDTH])

    qa_ref[...] = proj(0).astype(qa_ref.dtype)
    zf = proj(1)
    lb = lb_ref[...]
    e = jnp.exp(-jnp.abs(zf))
    r = 1.0 / (1.0 + e)
    sig_pos = jnp.where(zf >= 0, r, e * r)
    sig_neg = jnp.where(zf >= 0, e * r, r)
    lf_ref[...] = jnp.log(lb + (1.0 - lb) * sig_pos)
    ka_ref[...] = ((1.0 - lb) * sig_neg).astype(ka_ref.dtype)
    va_ref[...] = proj(2).astype(va_ref.dtype)
    zg_ref[...] = proj(3).astype(zg_ref.dtype)
    qb_ref[...] = proj(4).astype(qb_ref.dtype)
    tm = x.shape[0]
    for src, dst_ref, out_ref in ((proj(5), kb_ref, kout_ref), (proj(6), vb_ref, vout_ref)):
        dst_ref[...] = src.astype(dst_ref.dtype)
        for hd in range(N_HEADS):
            out_ref[pl.ds(hd, tm, stride=N_HEADS), :] = src[:, hd * HEAD_DIM:(hd + 1) * HEAD_DIM]


def _mixer_front(h, pre, w_front, lb, tm):
    n, d = h.shape
    row = pl.BlockSpec((tm, d), lambda i: (i, 0))
    col = pl.BlockSpec((tm, WIDTH), lambda i: (i, 0))
    half = jax.ShapeDtypeStruct((n, WIDTH), MXU_DTYPE)
    full = jax.ShapeDtypeStruct((n, WIDTH), F32)
    per_head = jax.ShapeDtypeStruct((n * N_HEADS, HEAD_DIM), F32)
    per_head_blk = pl.BlockSpec((tm * N_HEADS, HEAD_DIM), lambda i: (i, 0))
    return pl.pallas_call(
        _front_kernel,
        grid=(n // tm,),
        in_specs=[row, _const_spec((1, d)), _const_spec((d, 7 * WIDTH)),
                  _const_spec((1, WIDTH))],
        out_specs=[col] * 8 + [per_head_blk] * 2,
        out_shape=[half, half, half, full, half, half, half, half, per_head, per_head],
        compiler_params=_params("parallel"),
        name="mixer_front",
    )(h, pre, w_front, lb)


def _back_kernel(h_ref, oa_ref, zg_ref, ob_ref, pre_ref, post_ref, nw_ref,
                 wgate_ref, wa_ref, wb_ref, wo_ref, o_ref):
    x = h_ref[...]
    d = x.shape[1]
    xn = (_rms(x) * pre_ref[...]).astype(MXU_DTYPE)
    oa = oa_ref[...].astype(F32)
    heads = [_rms(oa[:, h * HEAD_DIM:(h + 1) * HEAD_DIM]) * nw_ref[...]
             for h in range(N_HEADS)]
    ya = (jnp.concatenate(heads, axis=1) * _silu(zg_ref[...].astype(F32))).astype(MXU_DTYPE)
    gate_a = jax.nn.sigmoid(_dot(xn, wgate_ref[:, :d]))
    gate_b = jax.nn.sigmoid(_dot(xn, wgate_ref[:, d:]))
    merged = gate_a * _dot(ya, wa_ref[...]) + gate_b * _dot(ob_ref[...], wb_ref[...])
    out = _dot(merged.astype(MXU_DTYPE), wo_ref[...])
    o_ref[...] = x + _rms(out) * post_ref[...]


def _mixer_back(h, oa, zg, ob, pre, post, nw, w_gate, wa, wb, wo, tm):
    n, d = h.shape
    row = pl.BlockSpec((tm, d), lambda i: (i, 0))
    col = pl.BlockSpec((tm, WIDTH), lambda i: (i, 0))
    return pl.pallas_call(
        _back_kernel,
        grid=(n // tm,),
        in_specs=[row, col, col, col, _const_spec((1, d)), _const_spec((1, d)),
                  _const_spec((1, HEAD_DIM)), _const_spec((d, 2 * d)),
                  _const_spec((WIDTH, d)), _const_spec((WIDTH, d)), _const_spec((d, d))],
        out_specs=row,
        out_shape=jax.ShapeDtypeStruct((n, d), F32),
        compiler_params=_params("parallel"),
        name="mixer_back",
    )(h, oa, zg, ob, pre, post, nw, w_gate, wa, wb, wo)


def _cumsum_rows(x):
    rows = x.shape[0]
    row = lax.broadcasted_iota(jnp.int32, x.shape, 0)
    shift = 1
    while shift < rows:
        x = x + jnp.where(row >= shift, pltpu.roll(x, shift, 0), 0.0)
        shift *= 2
    return x


def _scores_direct(q, k, cum):
    c = q.shape[0]
    col = lax.broadcasted_iota(jnp.int32, (c, c), 1)
    row = lax.broadcasted_iota(jnp.int32, (c, 1), 0)

    def body(s, acc):
        pick = row == s
        ks = jnp.sum(jnp.where(pick, k, 0.0), axis=0, keepdims=True)
        cs = jnp.sum(jnp.where(pick, cum, 0.0), axis=0, keepdims=True)
        t = q * ks * jnp.exp(jnp.minimum(cum - cs, 0.0))
        return jnp.where(col == s, jnp.sum(t, axis=1, keepdims=True), acc)

    return lax.fori_loop(0, c, body, jnp.zeros((c, c), F32))


def _gla_chunk(q, k, v, g, state, direct):
    c = q.shape[0]
    cum = _cumsum_rows(g)
    last = cum[c - 1:c, :]
    q_dec = (q * jnp.exp(cum)).astype(MXU_DTYPE)
    k_end = (k * jnp.exp(last - cum)).astype(MXU_DTYPE)
    v_mx = v.astype(MXU_DTYPE)
    state_decay = jnp.exp(last)
    if not direct:
        k_inv = (k * jnp.exp(-cum)).astype(MXU_DTYPE)
    tril = (lax.broadcasted_iota(jnp.int32, (c, c), 1)
            <= lax.broadcasted_iota(jnp.int32, (c, c), 0))
    outs, new_state = [], []
    for h in range(N_HEADS):
        sl = slice(h * HEAD_DIM, (h + 1) * HEAD_DIM)
        if direct:
            scores = _scores_direct(q[:, sl], k[:, sl], cum[:, sl])
        else:
            scores = _dot_nt(q_dec[:, sl], k_inv[:, sl])
        scores = jnp.where(tril, scores, 0.0).astype(MXU_DTYPE)
        outs.append(_dot_nt(q_dec[:, sl], state[h].astype(MXU_DTYPE)) + _dot(scores, v_mx[:, sl]))
        new_state.append(state[h] * state_decay[:, sl] + _dot_tn(v_mx[:, sl], k_end[:, sl]))
    return jnp.concatenate(outs, axis=1), new_state


def _gla_prompt_kernel(q_ref, k_ref, v_ref, g_ref, o_ref, s_ref):
    rows = q_ref.shape[0]
    n_chunks = (rows - N_META) // GLA_CHUNK

    def run(r0, c, state, direct):
        sl = pl.ds(r0, c)
        o, state = _gla_chunk(q_ref[sl, :].astype(F32), k_ref[sl, :].astype(F32),
                              v_ref[sl, :].astype(F32), g_ref[sl, :], state, direct)
        o_ref[sl, :] = o.astype(o_ref.dtype)
        return state

    def run_all(direct):
        state = [jnp.zeros((HEAD_DIM, HEAD_DIM), F32)] * N_HEADS
        state = run(0, N_META, state, direct)

        def body(i, state):
            state = list(state)
            for u in range(GLA_UNROLL):
                r0 = pl.multiple_of(N_META + (i * GLA_UNROLL + u) * GLA_CHUNK, 16)
                state = run(r0, GLA_CHUNK, state, direct)
            return tuple(state)

        state = lax.fori_loop(0, n_chunks // GLA_UNROLL, body, tuple(state))
        for h in range(N_HEADS):
            s_ref[0, h] = state[h].T

    g_main = g_ref[N_META:rows, :].reshape(n_chunks, GLA_CHUNK, WIDTH)
    decay = jnp.maximum(jnp.max(-jnp.sum(g_main, axis=1)),
                        jnp.max(-jnp.sum(g_ref[0:N_META, :], axis=0)))
    safe = decay <= GLA_SAFE_DECAY
    pl.when(safe)(functools.partial(run_all, False))
    pl.when(jnp.logical_not(safe))(functools.partial(run_all, True))


def _gla_prompt(qa, ka, va, lf, batch, seq_rows):
    blk = pl.BlockSpec((seq_rows, WIDTH), lambda b: (b, 0))
    return pl.pallas_call(
        _gla_prompt_kernel,
        grid=(batch,),
        in_specs=[blk] * 4,
        out_specs=[blk, pl.BlockSpec((1, N_HEADS, HEAD_DIM, HEAD_DIM), lambda b: (b, 0, 0, 0))],
        out_shape=[jax.ShapeDtypeStruct((batch * seq_rows, WIDTH), MXU_DTYPE),
                   jax.ShapeDtypeStruct((batch, N_HEADS, HEAD_DIM, HEAD_DIM), F32)],
        compiler_params=_params("parallel"),
        name="gla_prompt",
    )(qa, ka, va, lf)


def _gla_sample_kernel(q_ref, k_ref, v_ref, g_ref, s0_ref, o_ref, s_ref, *, steps):
    q = q_ref[...].astype(F32)
    k = k_ref[...].astype(F32)
    v = v_ref[...].astype(F32)
    g = g_ref[...]
    row = lax.broadcasted_iota(jnp.int32, q.shape, 0)

    def run_all(direct):
        out = jnp.zeros(q.shape, F32)
        for bi in range(ROWS_PER_STEP // steps):
            mine = (row >= bi * steps) & (row < (bi + 1) * steps)
            state = [s0_ref[bi, h].T for h in range(N_HEADS)]
            o, state = _gla_chunk(q, jnp.where(mine, k, 0.0), v, jnp.where(mine, g, 0.0),
                                  state, direct)
            out = jnp.where(mine, o, out)
            for h in range(N_HEADS):
                s_ref[bi, h] = state[h].T
        o_ref[...] = out.astype(o_ref.dtype)

    safe = jnp.max(-jnp.sum(g, axis=0)) <= GLA_SAFE_DECAY
    pl.when(safe)(functools.partial(run_all, False))
    pl.when(jnp.logical_not(safe))(functools.partial(run_all, True))


def _gla_sample(qa, ka, va, lf, states, layer, row0, dec_batch, steps):
    per = ROWS_PER_STEP // steps
    blk = pl.BlockSpec((ROWS_PER_STEP, WIDTH), lambda i: (row0 // ROWS_PER_STEP + i, 0))
    sblk = pl.BlockSpec((per, N_HEADS, HEAD_DIM, HEAD_DIM), lambda i: (i, 0, 0, 0))
    s0blk = pl.BlockSpec((None, per, N_HEADS, HEAD_DIM, HEAD_DIM), lambda i: (layer, i, 0, 0, 0))
    return pl.pallas_call(
        functools.partial(_gla_sample_kernel, steps=steps),
        grid=(dec_batch // per,),
        in_specs=[blk] * 4 + [s0blk],
        out_specs=[pl.BlockSpec((ROWS_PER_STEP, WIDTH), lambda i: (i, 0)), sblk],
        out_shape=[jax.ShapeDtypeStruct((dec_batch * steps, WIDTH), MXU_DTYPE),
                   jax.ShapeDtypeStruct(states.shape[1:], F32)],
        compiler_params=_params("parallel"),
        name="gla_sample",
    )(qa, ka, va, lf, states)


def _diff_finish(o1, o2, lam, scale, axis):
    d = o1 - lam * o2
    return d * lax.rsqrt(jnp.mean(d * d, axis=axis, keepdims=True) + RMS_EPS) * scale


def _attn_kernel(pt_ref, sc_ref, q_ref, k_ref, v_ref, qs_ref, kn_ref, vn_ref, ck_ref, cv_ref,
                 o_ref, os_ref, kbf_ref, vt_ref, s_ref, p_ref, kbuf, vbuf, sem,
                 *, layer, n_pages, dec_batch, steps):
    rows = q_ref.shape[0]
    n_full = rows // ATT_TILE
    rem = rows - n_full * ATT_TILE
    lam = sc_ref[0]
    out_scale = sc_ref[1]
    step_id = pl.program_id(0) * pl.num_programs(1) + pl.program_id(1)
    per = ROWS_PER_STEP // steps

    def copies(b, slot):
        out = []
        for p in range(n_pages):
            page = pt_ref[b * n_pages + p]
            dst = pl.ds(p * PAGE * N_HEADS, PAGE * N_HEADS)
            out.append(pltpu.make_async_copy(ck_ref.at[layer, page], kbuf.at[slot, dst],
                                             sem.at[slot, 0]))
            out.append(pltpu.make_async_copy(cv_ref.at[layer, page], vbuf.at[slot, dst],
                                             sem.at[slot, 1]))
        return out

    @pl.when(step_id == 0)
    def _():
        for c in copies(0, 0):
            c.start()

    kbf_ref[0:rows, :] = k_ref[...].astype(MXU_DTYPE)
    kbf_ref[rows:, :] = jnp.zeros((kbf_ref.shape[0] - rows, HEAD_DIM), MXU_DTYPE)
    for i in range(n_full):
        vt_ref[0:HEAD_DIM, i * ATT_TILE:(i + 1) * ATT_TILE] = (
            v_ref[i * ATT_TILE:(i + 1) * ATT_TILE, :].astype(F32).T.astype(MXU_DTYPE))
    v_tail = jnp.concatenate(
        [v_ref[n_full * ATT_TILE:rows, :].astype(F32),
         jnp.zeros((ATT_TILE - rem, HEAD_DIM), F32)], axis=0)
    vt_ref[0:HEAD_DIM, n_full * ATT_TILE:] = v_tail.T.astype(MXU_DTYPE)
    vt_ref[HEAD_DIM:, :] = jnp.ones((ONES_ROWS, vt_ref.shape[1]), MXU_DTYPE)

    def tile_scores(q0, tq, out_rows, slot):
        q_t = q_ref[q0:q0 + tq, :].astype(F32).T * (HALF_DIM ** -0.5 * LOG2_E)
        comp = lax.broadcasted_iota(jnp.int32, q_t.shape, 0) < HALF_DIM
        q_both = jnp.concatenate([jnp.where(comp, q_t, 0.0), jnp.where(comp, 0.0, q_t)],
                                 axis=1).astype(MXU_DTYPE)
        w = 2 * tq
        visible = q0 // ATT_TILE
        tiles = [(i * ATT_TILE, ATT_TILE, i >= visible)
                 for i in range(n_full) if i * ATT_TILE < q0 + tq]
        if n_full * ATT_TILE < q0 + tq:
            tiles.append((n_full * ATT_TILE, ATT_TAIL, True))
        k_tot = tiles[-1][0] + tiles[-1][1]

        m = None
        for kv0, tk, masked in tiles:
            s = _dot(kbf_ref[kv0:kv0 + tk, :], q_both)
            if masked:
                kpos = kv0 + lax.broadcasted_iota(jnp.int32, s.shape, 0)
                qcol = lax.broadcasted_iota(jnp.int32, s.shape, 1)
                qpos = q0 + jnp.where(qcol >= tq, qcol - tq, qcol)
                s = jnp.where(kpos <= qpos, s, NEG_BIG)
            s_ref[slot, kv0:kv0 + tk, 0:w] = s
            tile_max = jnp.max(s, axis=0, keepdims=True)
            m = tile_max if m is None else jnp.maximum(m, tile_max)
        return tiles, m

    def tile_finish(q0, tq, out_rows, slot, tiles, m):
        w = 2 * tq
        k_tot = tiles[-1][0] + tiles[-1][1]
        for kv0, tk, _ in tiles:
            p_ref[slot, kv0:kv0 + tk, 0:w] = jnp.exp2(
                s_ref[slot, kv0:kv0 + tk, 0:w] - m).astype(MXU_DTYPE)
        acc = _dot(vt_ref[:, 0:k_tot], p_ref[slot, 0:k_tot, 0:w])
        o = acc[:HEAD_DIM] / acc[HEAD_DIM:HEAD_DIM + 1]
        d = _diff_finish(o[:, :tq], o[:, tq:], lam, out_scale, 0)
        o_ref[q0 + tq - out_rows:q0 + tq, :] = d.T[tq - out_rows:, :].astype(o_ref.dtype)

    def heads_to_lanes(buf, slot):
        past = n_pages * PAGE
        return jnp.concatenate(
            [buf[slot, pl.ds(h, past, stride=N_HEADS), :]
             for h in range(N_HEADS)], axis=1)

    q_all = qs_ref[...].astype(F32) * (HALF_DIM ** -0.5)
    k_new = kn_ref[...].astype(MXU_DTYPE)
    v_new = vn_ref[...].astype(MXU_DTYPE)
    n_q = 2 * N_HEADS * 8
    g_row = lax.broadcasted_iota(jnp.int32, (n_q, WIDTH), 0) // 8
    g_lane = lax.broadcasted_iota(jnp.int32, (n_q, WIDTH), 1) // HALF_DIM
    new_row = lax.broadcasted_iota(jnp.int32, (n_q, ROWS_PER_STEP), 0) % 8
    new_col = lax.broadcasted_iota(jnp.int32, (n_q, ROWS_PER_STEP), 1)

    def decode(bi):
        b = step_id * per + bi
        slot = bi % 2

        @pl.when(b + 1 < dec_batch)
        def _():
            for c in copies(b + 1, 1 - slot):
                c.start()

        for c in copies(b, slot):
            c.wait()

        q8 = q_all[8 * (bi * steps // 8):8 * (bi * steps // 8) + 8, :]
        q_bd = jnp.where(g_row == g_lane, jnp.concatenate([q8] * (2 * N_HEADS), axis=0), 0.0)
        s_past = _dot_nt(q_bd, heads_to_lanes(kbuf, slot))
        s_new = _dot_nt(q_bd.astype(MXU_DTYPE), k_new)
        new_ok = (new_col // steps == bi) & (new_col % steps <= new_row % steps)
        s_new = jnp.where(new_ok, s_new, NEG_BIG)
        m = jnp.maximum(jnp.max(s_past, axis=1, keepdims=True),
                        jnp.max(s_new, axis=1, keepdims=True))
        p_past = jnp.exp(s_past - m)
        p_new = jnp.exp(s_new - m)
        l = jnp.sum(p_past, axis=1, keepdims=True) + jnp.sum(p_new, axis=1, keepdims=True)
        o = (_dot(p_past, heads_to_lanes(vbuf, slot))
             + _dot(p_new.astype(MXU_DTYPE), v_new)) / l
        heads = []
        for h in range(N_HEADS):
            sl = slice(h * HEAD_DIM, (h + 1) * HEAD_DIM)
            heads.append(_diff_finish(o[16 * h:16 * h + 8, sl], o[16 * h + 8:16 * h + 16, sl],
                                      lam, out_scale, 1))
        return jnp.concatenate(heads, axis=1)

    q_tiles = [(j * ATT_TILE, ATT_TILE, ATT_TILE) for j in range(n_full)]
    if rem:
        q_tiles.append((rows - ATT_TAIL, ATT_TAIL, rem))
    share = -(-len(q_tiles) // per)
    results = [decode(0)]
    pending = tile_scores(*q_tiles[0], 0)
    for j in range(len(q_tiles)):
        ahead = tile_scores(*q_tiles[j + 1], (j + 1) % 2) if j + 1 < len(q_tiles) else None
        tile_finish(*q_tiles[j], j % 2, *pending)
        pending = ahead
        if (j + 1) % share == 0 and len(results) < per:
            results.append(decode(len(results)))
    while len(results) < per:
        results.append(decode(len(results)))

    row8 = lax.broadcasted_iota(jnp.int32, (8, WIDTH), 0)
    merged = []
    for pair in range(ROWS_PER_STEP // 8):
        acc = results[pair * (8 // steps)]
        for j in range(1, 8 // steps):
            acc = jnp.where(row8 // steps == j, results[pair * (8 // steps) + j], acc)
        merged.append(acc)
    os_ref[...] = jnp.concatenate(merged, axis=0).astype(os_ref.dtype)


def _attention(page_table, scalars, qb, kb, vb, cache_k, cache_v, layer, batch, seq_rows,
               dec_batch, steps):
    assert seq_rows >= ATT_TILE and seq_rows % ATT_TILE <= ATT_TAIL
    assert 8 % steps == 0 and dec_batch * steps == batch * N_HEADS * ROWS_PER_STEP
    n_full = seq_rows // ATT_TILE
    kv_rows = n_full * ATT_TILE + ATT_TAIL
    n_pages = page_table.shape[1]
    sample0 = batch * seq_rows // ROWS_PER_STEP
    blk = pl.BlockSpec((seq_rows, HEAD_DIM), lambda b, h, pt: (b, h))
    sblk = pl.BlockSpec((ROWS_PER_STEP, WIDTH), lambda b, h, pt: (sample0 + b * N_HEADS + h, 0))
    grid_spec = pltpu.PrefetchScalarGridSpec(
        num_scalar_prefetch=1,
        grid=(batch, N_HEADS),
        in_specs=[pl.BlockSpec(memory_space=pltpu.SMEM), blk, blk, blk, sblk, sblk, sblk,
                  pl.BlockSpec(memory_space=pl.ANY), pl.BlockSpec(memory_space=pl.ANY)],
        out_specs=[blk, pl.BlockSpec((ROWS_PER_STEP, WIDTH), lambda b, h, pt: (b * N_HEADS + h, 0))],
        scratch_shapes=[pltpu.VMEM((kv_rows, HEAD_DIM), MXU_DTYPE),
                        pltpu.VMEM((HEAD_DIM + ONES_ROWS, (n_full + 1) * ATT_TILE), MXU_DTYPE),
                        pltpu.VMEM((2, kv_rows, 2 * ATT_TILE), F32),
                        pltpu.VMEM((2, kv_rows, 2 * ATT_TILE), MXU_DTYPE),
                        pltpu.VMEM((2, n_pages * PAGE * N_HEADS, HEAD_DIM), F32),
                        pltpu.VMEM((2, n_pages * PAGE * N_HEADS, HEAD_DIM), F32),
                        pltpu.SemaphoreType.DMA((2, 2))],
    )
    return pl.pallas_call(
        functools.partial(_attn_kernel, layer=layer, n_pages=n_pages, dec_batch=dec_batch,
                          steps=steps),
        grid_spec=grid_spec,
        out_shape=[jax.ShapeDtypeStruct((batch * seq_rows, WIDTH), MXU_DTYPE),
                   jax.ShapeDtypeStruct((dec_batch * steps, WIDTH), MXU_DTYPE)],
        compiler_params=_params("arbitrary", "arbitrary"),
        name="attention",
    )(page_table.reshape(-1), scalars, qb, kb, vb, qb, kb, vb, cache_k, cache_v)


def _collect_kernel(*refs, depth):
    k_srcs, v_srcs = refs[:depth], refs[depth:2 * depth]
    k_dst, v_dst = refs[2 * depth:]
    layer = pl.program_id(0)
    for l in range(depth):
        @pl.when(layer == l)
        def _():
            k_dst[...] = k_srcs[l][...]
            v_dst[...] = v_srcs[l][...]


def _collect(k_outs, v_outs, rows, block_rows):
    depth = len(k_outs)
    assert rows % block_rows == 0 and block_rows % 8 == 0
    n_blocks = rows // block_rows

    def src_spec(l):
        return pl.BlockSpec(
            (block_rows, HEAD_DIM),
            lambda d, i: (jnp.where(d == l, i, jnp.where(d < l, 0, n_blocks - 1)), 0))

    dst_spec = pl.BlockSpec((None, block_rows, HEAD_DIM), lambda d, i: (d, i, 0))
    stacked = jax.ShapeDtypeStruct((depth, rows, HEAD_DIM), F32)
    return pl.pallas_call(
        functools.partial(_collect_kernel, depth=depth),
        grid=(depth, n_blocks),
        in_specs=[src_spec(l) for l in range(depth)] * 2,
        out_specs=[dst_spec, dst_spec],
        out_shape=[stacked, stacked],
        compiler_params=_params("arbitrary", "arbitrary"),
        name="collect_kv",
    )(*k_outs, *v_outs)


def kernel(x_prompt, x_sample, cache_k, cache_v, state_hgrn, page_table, meta_tokens, ffn1_pre_norm, ffn1_post_norm, ffn1_w_gate, ffn1_w_up, ffn1_w_down, mix_pre_norm, mix_post_norm, w_in, hgrn_lb_logits, hgrn_norm_w, lambda_q1, lambda_k1, lambda_q2, lambda_k2, w_branch_a, w_branch_b, w_out, ffn2_pre_norm, ffn2_post_norm, ffn2_w_gate, ffn2_w_up, ffn2_w_down):
    batch, seq, d = x_prompt.shape
    dec_batch, steps = x_sample.shape[:2]
    depth = w_in.shape[0]
    seq_rows = N_META + seq
    n_prompt = batch * seq_rows
    n_sample = dec_batch * steps
    n = n_prompt + n_sample
    assert n_prompt % ROWS_PER_STEP == 0 and ROWS_PER_STEP % steps == 0
    assert (seq_rows - N_META) % (GLA_CHUNK * GLA_UNROLL) == 0
    tm = _row_tile(n, ROW_TILE_TARGET)

    lbs = jnp.cumsum(jax.nn.softmax(hgrn_lb_logits.astype(F32), axis=0), axis=0)
    lbs = lbs - lbs[0:1]
    lam_init = jnp.asarray([0.8 - 0.6 * math.exp(-0.3 * l) for l in range(depth)], F32)
    lam = (jnp.exp(jnp.sum(lambda_q1.astype(F32) * lambda_k1.astype(F32), axis=-1))
           - jnp.exp(jnp.sum(lambda_q2.astype(F32) * lambda_k2.astype(F32), axis=-1)) + lam_init)
    attn_scalars = jnp.stack([lam, 1.0 - lam_init], axis=1)

    meta = meta_tokens.astype(x_prompt.dtype)
    h = jnp.concatenate([piece for b in range(batch) for piece in (meta, x_prompt[b])]
                        + [x_sample.reshape(n_sample, d)], axis=0)
    cache_k = cache_k.reshape(cache_k.shape[:2] + (PAGE * N_HEADS, HEAD_DIM))
    cache_v = cache_v.reshape(cache_v.shape[:2] + (PAGE * N_HEADS, HEAD_DIM))
    n_front = 7 * WIDTH

    def cast(w):
        return w.astype(MXU_DTYPE)

    def vec(w):
        return w.reshape(1, -1).astype(F32)

    k_outs, v_outs, st_p, st_s = [], [], [], []
    for l in range(depth):
        h = _ffn_half(h, vec(ffn1_pre_norm[l]), vec(ffn1_post_norm[l]), cast(ffn1_w_gate[l]),
                      cast(ffn1_w_up[l]), cast(ffn1_w_down[l]), tm)
        qa, ka, va, lf, zg, qb, kb, vb, k_out, v_out = _mixer_front(
            h, vec(mix_pre_norm[l]), cast(w_in[l, :, :n_front]), vec(lbs[l]), tm)
        oa_p, s_p = _gla_prompt(qa, ka, va, lf, batch, seq_rows)
        oa_s, s_s = _gla_sample(qa, ka, va, lf, state_hgrn, l, n_prompt,
                                dec_batch, steps)
        ob_p, ob_s = _attention(page_table, attn_scalars[l], qb, kb, vb, cache_k, cache_v, l,
                                batch, seq_rows, dec_batch, steps)
        h = _mixer_back(h, jnp.concatenate([oa_p, oa_s], axis=0), zg,
                        jnp.concatenate([ob_p, ob_s], axis=0), vec(mix_pre_norm[l]),
                        vec(mix_post_norm[l]), vec(hgrn_norm_w[l]), cast(w_in[l, :, n_front:]),
                        cast(w_branch_a[l]), cast(w_branch_b[l]), cast(w_out[l]), tm)
        h = _ffn_half(h, vec(ffn2_pre_norm[l]), vec(ffn2_post_norm[l]), cast(ffn2_w_gate[l]),
                      cast(ffn2_w_up[l]), cast(ffn2_w_down[l]), tm)
        k_outs.append(k_out)
        v_outs.append(v_out)
        st_p.append(s_p.astype(state_hgrn.dtype))
        st_s.append(s_s.astype(state_hgrn.dtype))
    split = n_prompt * N_HEADS
    k_p, v_p = _collect(k_outs, v_outs, split, seq_rows)
    k_s = jnp.stack([k[split:] for k in k_outs])
    v_s = jnp.stack([v[split:] for v in v_outs])
    prompt_shape = (depth, batch, seq_rows, N_HEADS, HEAD_DIM)
    sample_shape = (depth, dec_batch, steps, N_HEADS, HEAD_DIM)
    y_prompt = h[:n_prompt].reshape(batch, seq_rows, d)[:, N_META:]
    y_sample = h[n_prompt:].reshape(dec_batch, steps, d)
    return (y_prompt, y_sample, k_p.reshape(prompt_shape), v_p.reshape(prompt_shape),
            jnp.stack(st_p), k_s.reshape(sample_shape), v_s.reshape(sample_shape),
            jnp.stack(st_s))
```

```python
import functools
import math

import jax
import jax.numpy as jnp
from jax import lax
from jax.experimental import pallas as pl
from jax.experimental.pallas import tpu as pltpu

F32 = jnp.float32
MXU_DTYPE = jnp.bfloat16
RMS_EPS = 1e-6
N_META = 16
N_HEADS = 4
HEAD_DIM = 128
HALF_DIM = HEAD_DIM // 2
WIDTH = N_HEADS * HEAD_DIM
PAGE = 128
GLA_CHUNK = 64
GLA_UNROLL = 8
GLA_SAFE_DECAY = 80.0
ATT_TILE = 256
ATT_TAIL = 128
ONES_ROWS = 16
LOG2_E = 1.4426950408889634
ROWS_PER_STEP = 16
FF_CHUNK = 1024
ROW_TILE_TARGET = 896
NEG_BIG = -1e30
VMEM_LIMIT = 56 * 1024 * 1024


def _rms(x):
    return x * lax.rsqrt(jnp.mean(x * x, axis=-1, keepdims=True) + RMS_EPS)


def _dot(a, b):
    return jnp.dot(a, b, preferred_element_type=F32)


def _dot_nt(a, b):
    return lax.dot_general(a, b, (((1,), (1,)), ((), ())), preferred_element_type=F32)


def _dot_tn(a, b):
    return lax.dot_general(a, b, (((0,), (0,)), ((), ())), preferred_element_type=F32)


def _silu(x):
    return x * jax.nn.sigmoid(x)


def _row_tile(n, target, align=16):
    best = None
    for t in range(align, min(n, target) + 1, align):
        if n % t == 0:
            best = t
    assert best is not None, (n, target)
    return best


def _chunks(total, size):
    return tuple((c, min(c + size, total)) for c in range(0, total, size))


def _params(*sem):
    return pltpu.CompilerParams(dimension_semantics=sem, vmem_limit_bytes=VMEM_LIMIT)


def _const_spec(shape):
    return pl.BlockSpec(shape, lambda *_: (0,) * len(shape), pipeline_mode=pl.Buffered(1))


def _ffn_kernel(h_ref, pre_ref, post_ref, wg_ref, wu_ref, wd_ref, o_ref, *, chunks):
    x = h_ref[...]
    xn = (_rms(x) * pre_ref[...]).astype(MXU_DTYPE)
    y = None
    for c0, c1 in chunks:
        g = _dot(xn, wg_ref[:, c0:c1])
        u = _dot(xn, wu_ref[:, c0:c1])
        part = _dot((_silu(g) * u).astype(MXU_DTYPE), wd_ref[c0:c1, :])
        y = part if y is None else y + part
    o_ref[...] = x + 0.5 * (_rms(y) * post_ref[...])


def _ffn_half(h, pre, post, wg, wu, wd, tm):
    n, d = h.shape
    f = wg.shape[1]
    row = pl.BlockSpec((tm, d), lambda i: (i, 0))
    return pl.pallas_call(
        functools.partial(_ffn_kernel, chunks=_chunks(f, FF_CHUNK)),
        grid=(n // tm,),
        in_specs=[row, _const_spec((1, d)), _const_spec((1, d)),
                  _const_spec((d, f)), _const_spec((d, f)), _const_spec((f, d))],
        out_specs=row,
        out_shape=jax.ShapeDtypeStruct((n, d), F32),
        compiler_params=_params("parallel"),
        name="ffn_half",
    )(h, pre, post, wg, wu, wd)


def _front_kernel(h_ref, pre_ref, w_ref, lb_ref,
                  qa_ref, ka_ref, va_ref, lf_ref, zg_ref, qb_ref, kb_ref, vb_ref,
                  kout_ref, vout_ref):
    x = h_ref[...]
    xn = (_rms(x) * pre_ref[...]).astype(MXU_DTYPE)

    def proj(i):
        return _dot(xn, w_ref[:, i * WIDTH:(i + 1) * WIDTH])

    qa_ref[...] = proj(0).astype(qa_ref.dtype)
    zf = proj(1)
    lb = lb_ref[...]
    e = jnp.exp(-jnp.abs(zf))
    r = 1.0 / (1.0 + e)
    sig_pos = jnp.where(zf >= 0, r, e * r)
    sig_neg = jnp.where(zf >= 0, e * r, r)
    lf_ref[...] = jnp.log(lb + (1.0 - lb) * sig_pos)
    ka_ref[...] = ((1.0 - lb) * sig_neg).astype(ka_ref.dtype)
    va_ref[...] = proj(2).astype(va_ref.dtype)
    zg_ref[...] = proj(3).astype(zg_ref.dtype)
    qb_ref[...] = proj(4).astype(qb_ref.dtype)
    tm = x.shape[0]
    for src, dst_ref, out_ref in ((proj(5), kb_ref, kout_ref), (proj(6), vb_ref, vout_ref)):
        dst_ref[...] = src.astype(dst_ref.dtype)
        for hd in range(N_HEADS):
            out_ref[pl.ds(hd, tm, stride=N_HEADS), :] = src[:, hd * HEAD_DIM:(hd + 1) * HEAD_DIM]


def _mixer_front(h, pre, w_front, lb, tm):
    n, d = h.shape
    row = pl.BlockSpec((tm, d), lambda i: (i, 0))
    col = pl.BlockSpec((tm, WIDTH), lambda i: (i, 0))
    half = jax.ShapeDtypeStruct((n, WIDTH), MXU_DTYPE)
    full = jax.ShapeDtypeStruct((n, WIDTH), F32)
    per_head = jax.ShapeDtypeStruct((n * N_HEADS, HEAD_DIM), F32)
    per_head_blk = pl.BlockSpec((tm * N_HEADS, HEAD_DIM), lambda i: (i, 0))
    return pl.pallas_call(
        _front_kernel,
        grid=(n // tm,),
        in_specs=[row, _const_spec((1, d)), _const_spec((d, 7 * WIDTH)),
                  _const_spec((1, WIDTH))],
        out_specs=[col] * 8 + [per_head_blk] * 2,
        out_shape=[half, half, half, full, half, half, half, half, per_head, per_head],
        compiler_params=_params("parallel"),
        name="mixer_front",
    )(h, pre, w_front, lb)


def _back_kernel(h_ref, oa_ref, zg_ref, ob_ref, pre_ref, post_ref, nw_ref,
                 wgate_ref, wa_ref, wb_ref, wo_ref, o_ref):
    x = h_ref[...]
    d = x.shape[1]
    xn = (_rms(x) * pre_ref[...]).astype(MXU_DTYPE)
    oa = oa_ref[...].astype(F32)
    heads = [_rms(oa[:, h * HEAD_DIM:(h + 1) * HEAD_DIM]) * nw_ref[...]
             for h in range(N_HEADS)]
    ya = (jnp.concatenate(heads, axis=1) * _silu(zg_ref[...].astype(F32))).astype(MXU_DTYPE)
    gate_a = jax.nn.sigmoid(_dot(xn, wgate_ref[:, :d]))
    gate_b = jax.nn.sigmoid(_dot(xn, wgate_ref[:, d:]))
    merged = gate_a * _dot(ya, wa_ref[...]) + gate_b * _dot(ob_ref[...], wb_ref[...])
    out = _dot(merged.astype(MXU_DTYPE), wo_ref[...])
    o_ref[...] = x + _rms(out) * post_ref[...]


def _mixer_back(h, oa, zg, ob, pre, post, nw, w_gate, wa, wb, wo, tm):
    n, d = h.shape
    row = pl.BlockSpec((tm, d), lambda i: (i, 0))
    col = pl.BlockSpec((tm, WIDTH), lambda i: (i, 0))
    return pl.pallas_call(
        _back_kernel,
        grid=(n // tm,),
        in_specs=[row, col, col, col, _const_spec((1, d)), _const_spec((1, d)),
                  _const_spec((1, HEAD_DIM)), _const_spec((d, 2 * d)),
                  _const_spec((WIDTH, d)), _const_spec((WIDTH, d)), _const_spec((d, d))],
        out_specs=row,
        out_shape=jax.ShapeDtypeStruct((n, d), F32),
        compiler_params=_params("parallel"),
        name="mixer_back",
    )(h, oa, zg, ob, pre, post, nw, w_gate, wa, wb, wo)


def _cumsum_rows(x):
    rows = x.shape[0]
    row = lax.broadcasted_iota(jnp.int32, x.shape, 0)
    shift = 1
    while shift < rows:
        x = x + jnp.where(row >= shift, pltpu.roll(x, shift, 0), 0.0)
        shift *= 2
    return x


def _scores_direct(q, k, cum):
    c = q.shape[0]
    col = lax.broadcasted_iota(jnp.int32, (c, c), 1)
    row = lax.broadcasted_iota(jnp.int32, (c, 1), 0)

    def body(s, acc):
        pick = row == s
        ks = jnp.sum(jnp.where(pick, k, 0.0), axis=0, keepdims=True)
        cs = jnp.sum(jnp.where(pick, cum, 0.0), axis=0, keepdims=True)
        t = q * ks * jnp.exp(jnp.minimum(cum - cs, 0.0))
        return jnp.where(col == s, jnp.sum(t, axis=1, keepdims=True), acc)

    return lax.fori_loop(0, c, body, jnp.zeros((c, c), F32))


def _gla_chunk(q, k, v, g, state, direct):
    c = q.shape[0]
    cum = _cumsum_rows(g)
    last = cum[c - 1:c, :]
    q_dec = (q * jnp.exp(cum)).astype(MXU_DTYPE)
    k_end = (k * jnp.exp(last - cum)).astype(MXU_DTYPE)
    v_mx = v.astype(MXU_DTYPE)
    state_decay = jnp.exp(last)
    if not direct:
        k_inv = (k * jnp.exp(-cum)).astype(MXU_DTYPE)
    tril = (lax.broadcasted_iota(jnp.int32, (c, c), 1)
            <= lax.broadcasted_iota(jnp.int32, (c, c), 0))
    outs, new_state = [], []
    for h in range(N_HEADS):
        sl = slice(h * HEAD_DIM, (h + 1) * HEAD_DIM)
        if direct:
            scores = _scores_direct(q[:, sl], k[:, sl], cum[:, sl])
        else:
            scores = _dot_nt(q_dec[:, sl], k_inv[:, sl])
        scores = jnp.where(tril, scores, 0.0).astype(MXU_DTYPE)
        outs.append(_dot_nt(q_dec[:, sl], state[h].astype(MXU_DTYPE)) + _dot(scores, v_mx[:, sl]))
        new_state.append(state[h] * state_decay[:, sl] + _dot_tn(v_mx[:, sl], k_end[:, sl]))
    return jnp.concatenate(outs, axis=1), new_state


def _gla_prompt_kernel(q_ref, k_ref, v_ref, g_ref, o_ref, s_ref):
    rows = q_ref.shape[0]
    n_chunks = (rows - N_META) // GLA_CHUNK

    def run(r0, c, state, direct):
        sl = pl.ds(r0, c)
        o, state = _gla_chunk(q_ref[sl, :].astype(F32), k_ref[sl, :].astype(F32),
                              v_ref[sl, :].astype(F32), g_ref[sl, :], state, direct)
        o_ref[sl, :] = o.astype(o_ref.dtype)
        return state

    def run_all(direct):
        state = [jnp.zeros((HEAD_DIM, HEAD_DIM), F32)] * N_HEADS
        state = run(0, N_META, state, direct)

        def body(i, state):
            state = list(state)
            for u in range(GLA_UNROLL):
                r0 = pl.multiple_of(N_META + (i * GLA_UNROLL + u) * GLA_CHUNK, 16)
                state = run(r0, GLA_CHUNK, state, direct)
            return tuple(state)

        state = lax.fori_loop(0, n_chunks // GLA_UNROLL, body, tuple(state))
        for h in range(N_HEADS):
            s_ref[0, h] = state[h].T

    g_main = g_ref[N_META:rows, :].reshape(n_chunks, GLA_CHUNK, WIDTH)
    decay = jnp.maximum(jnp.max(-jnp.sum(g_main, axis=1)),
                        jnp.max(-jnp.sum(g_ref[0:N_META, :], axis=0)))
    safe = decay <= GLA_SAFE_DECAY
    pl.when(safe)(functools.partial(run_all, False))
    pl.when(jnp.logical_not(safe))(functools.partial(run_all, True))


def _gla_prompt(qa, ka, va, lf, batch, seq_rows):
    blk = pl.BlockSpec((seq_rows, WIDTH), lambda b: (b, 0))
    return pl.pallas_call(
        _gla_prompt_kernel,
        grid=(batch,),
        in_specs=[blk] * 4,
        out_specs=[blk, pl.BlockSpec((1, N_HEADS, HEAD_DIM, HEAD_DIM), lambda b: (b, 0, 0, 0))],
        out_shape=[jax.ShapeDtypeStruct((batch * seq_rows, WIDTH), MXU_DTYPE),
                   jax.ShapeDtypeStruct((batch, N_HEADS, HEAD_DIM, HEAD_DIM), F32)],
        compiler_params=_params("parallel"),
        name="gla_prompt",
    )(qa, ka, va, lf)


def _gla_sample_kernel(q_ref, k_ref, v_ref, g_ref, s0_ref, o_ref, s_ref, *, steps):
    q = q_ref[...].astype(F32)
    k = k_ref[...].astype(F32)
    v = v_ref[...].astype(F32)
    g = g_ref[...]
    row = lax.broadcasted_iota(jnp.int32, q.shape, 0)

    def run_all(direct):
        out = jnp.zeros(q.shape, F32)
        for bi in range(ROWS_PER_STEP // steps):
            mine = (row >= bi * steps) & (row < (bi + 1) * steps)
            state = [s0_ref[bi, h].T for h in range(N_HEADS)]
            o, state = _gla_chunk(q, jnp.where(mine, k, 0.0), v, jnp.where(mine, g, 0.0),
                                  state, direct)
            out = jnp.where(mine, o, out)
            for h in range(N_HEADS):
                s_ref[bi, h] = state[h].T
        o_ref[...] = out.astype(o_ref.dtype)

    safe = jnp.max(-jnp.sum(g, axis=0)) <= GLA_SAFE_DECAY
    pl.when(safe)(functools.partial(run_all, False))
    pl.when(jnp.logical_not(safe))(functools.partial(run_all, True))


def _gla_sample(qa, ka, va, lf, states, layer, row0, dec_batch, steps):
    per = ROWS_PER_STEP // steps
    blk = pl.BlockSpec((ROWS_PER_STEP, WIDTH), lambda i: (row0 // ROWS_PER_STEP + i, 0))
    sblk = pl.BlockSpec((per, N_HEADS, HEAD_DIM, HEAD_DIM), lambda i: (i, 0, 0, 0))
    s0blk = pl.BlockSpec((None, per, N_HEADS, HEAD_DIM, HEAD_DIM), lambda i: (layer, i, 0, 0, 0))
    return pl.pallas_call(
        functools.partial(_gla_sample_kernel, steps=steps),
        grid=(dec_batch // per,),
        in_specs=[blk] * 4 + [s0blk],
        out_specs=[pl.BlockSpec((ROWS_PER_STEP, WIDTH), lambda i: (i, 0)), sblk],
        out_shape=[jax.ShapeDtypeStruct((dec_batch * steps, WIDTH), MXU_DTYPE),
                   jax.ShapeDtypeStruct(states.shape[1:], F32)],
        compiler_params=_params("parallel"),
        name="gla_sample",
    )(qa, ka, va, lf, states)


def _diff_finish(o1, o2, lam, scale, axis):
    d = o1 - lam * o2
    return d * lax.rsqrt(jnp.mean(d * d, axis=axis, keepdims=True) + RMS_EPS) * scale


def _attn_kernel(pt_ref, sc_ref, q_ref, k_ref, v_ref, qs_ref, kn_ref, vn_ref, ck_ref, cv_ref,
                 o_ref, os_ref, kbf_ref, vt_ref, s_ref, p_ref, kbuf, vbuf, sem,
                 *, layer, n_pages, dec_batch, steps):
    rows = q_ref.shape[0]
    n_full = rows // ATT_TILE
    rem = rows - n_full * ATT_TILE
    lam = sc_ref[0]
    out_scale = sc_ref[1]
    step_id = pl.program_id(0) * pl.num_programs(1) + pl.program_id(1)
    per = ROWS_PER_STEP // steps

    def copies(b, slot):
        out = []
        for p in range(n_pages):
            page = pt_ref[b * n_pages + p]
            dst = pl.ds(p * PAGE * N_HEADS, PAGE * N_HEADS)
            out.append(pltpu.make_async_copy(ck_ref.at[layer, page], kbuf.at[slot, dst],
                                             sem.at[slot, 0]))
            out.append(pltpu.make_async_copy(cv_ref.at[layer, page], vbuf.at[slot, dst],
                                             sem.at[slot, 1]))
        return out

    @pl.when(step_id == 0)
    def _():
        for c in copies(0, 0):
            c.start()

    kbf_ref[0:rows, :] = k_ref[...].astype(MXU_DTYPE)
    kbf_ref[rows:, :] = jnp.zeros((kbf_ref.shape[0] - rows, HEAD_DIM), MXU_DTYPE)
    for i in range(n_full):
        vt_ref[0:HEAD_DIM, i * ATT_TILE:(i + 1) * ATT_TILE] = (
            v_ref[i * ATT_TILE:(i + 1) * ATT_TILE, :].astype(F32).T.astype(MXU_DTYPE))
    v_tail = jnp.concatenate(
        [v_ref[n_full * ATT_TILE:rows, :].astype(F32),
         jnp.zeros((ATT_TILE - rem, HEAD_DIM), F32)], axis=0)
    vt_ref[0:HEAD_DIM, n_full * ATT_TILE:] = v_tail.T.astype(MXU_DTYPE)
    vt_ref[HEAD_DIM:, :] = jnp.ones((ONES_ROWS, vt_ref.shape[1]), MXU_DTYPE)

    def tile_scores(q0, tq, out_rows, slot):
        q_t = q_ref[q0:q0 + tq, :].astype(F32).T * (HALF_DIM ** -0.5 * LOG2_E)
        comp = lax.broadcasted_iota(jnp.int32, q_t.shape, 0) < HALF_DIM
        q_both = jnp.concatenate([jnp.where(comp, q_t, 0.0), jnp.where(comp, 0.0, q_t)],
                                 axis=1).astype(MXU_DTYPE)
        w = 2 * tq
        visible = q0 // ATT_TILE
        tiles = [(i * ATT_TILE, ATT_TILE, i >= visible)
                 for i in range(n_full) if i * ATT_TILE < q0 + tq]
        if n_full * ATT_TILE < q0 + tq:
            tiles.append((n_full * ATT_TILE, ATT_TAIL, True))
        k_tot = tiles[-1][0] + tiles[-1][1]

        m = None
        for kv0, tk, masked in tiles:
            s = _dot(kbf_ref[kv0:kv0 + tk, :], q_both)
            if masked:
                kpos = kv0 + lax.broadcasted_iota(jnp.int32, s.shape, 0)
                qcol = lax.broadcasted_iota(jnp.int32, s.shape, 1)
                qpos = q0 + jnp.where(qcol >= tq, qcol - tq, qcol)
                s = jnp.where(kpos <= qpos, s, NEG_BIG)
            s_ref[slot, kv0:kv0 + tk, 0:w] = s
            tile_max = jnp.max(s, axis=0, keepdims=True)
            m = tile_max if m is None else jnp.maximum(m, tile_max)
        return tiles, m

    def tile_finish(q0, tq, out_rows, slot, tiles, m):
        w = 2 * tq
        k_tot = tiles[-1][0] + tiles[-1][1]
        for kv0, tk, _ in tiles:
            p_ref[slot, kv0:kv0 + tk, 0:w] = jnp.exp2(
                s_ref[slot, kv0:kv0 + tk, 0:w] - m).astype(MXU_DTYPE)
        acc = _dot(vt_ref[:, 0:k_tot], p_ref[slot, 0:k_tot, 0:w])
        o = acc[:HEAD_DIM] / acc[HEAD_DIM:HEAD_DIM + 1]
        d = _diff_finish(o[:, :tq], o[:, tq:], lam, out_scale, 0)
        o_ref[q0 + tq - out_rows:q0 + tq, :] = d.T[tq - out_rows:, :].astype(o_ref.dtype)

    def heads_to_lanes(buf, slot):
        past = n_pages * PAGE
        return jnp.concatenate(
            [buf[slot, pl.ds(h, past, stride=N_HEADS), :]
             for h in range(N_HEADS)], axis=1)

    q_all = qs_ref[...].astype(F32) * (HALF_DIM ** -0.5 * LOG2_E)
    k_new = kn_ref[...].astype(MXU_DTYPE)
    v_new = vn_ref[...].astype(MXU_DTYPE)
    n_q = 2 * N_HEADS * 8
    g_row = lax.broadcasted_iota(jnp.int32, (n_q, WIDTH), 0) // 8
    g_lane = lax.broadcasted_iota(jnp.int32, (n_q, WIDTH), 1) // HALF_DIM
    new_row = lax.broadcasted_iota(jnp.int32, (n_q, ROWS_PER_STEP), 0) % 8
    new_col = lax.broadcasted_iota(jnp.int32, (n_q, ROWS_PER_STEP), 1)

    def decode(bi):
        b = step_id * per + bi
        slot = bi % 2

        @pl.when(b + 1 < dec_batch)
        def _():
            for c in copies(b + 1, 1 - slot):
                c.start()

        for c in copies(b, slot):
            c.wait()

        q8 = q_all[8 * (bi * steps // 8):8 * (bi * steps // 8) + 8, :]
        q_bd = jnp.where(g_row == g_lane, jnp.concatenate([q8] * (2 * N_HEADS), axis=0), 0.0)
        s_past = _dot_nt(q_bd, heads_to_lanes(kbuf, slot))
        s_new = _dot_nt(q_bd.astype(MXU_DTYPE), k_new)
        new_ok = (new_col // steps == bi) & (new_col % steps <= new_row % steps)
        s_new = jnp.where(new_ok, s_new, NEG_BIG)
        m = jnp.maximum(jnp.max(s_past, axis=1, keepdims=True),
                        jnp.max(s_new, axis=1, keepdims=True))
        p_past = jnp.exp2(s_past - m)
        p_new = jnp.exp2(s_new - m)
        l = jnp.sum(p_past, axis=1, keepdims=True) + jnp.sum(p_new, axis=1, keepdims=True)
        o = (_dot(p_past, heads_to_lanes(vbuf, slot))
             + _dot(p_new.astype(MXU_DTYPE), v_new)) / l
        heads = []
        for h in range(N_HEADS):
            sl = slice(h * HEAD_DIM, (h + 1) * HEAD_DIM)
            heads.append(_diff_finish(o[16 * h:16 * h + 8, sl], o[16 * h + 8:16 * h + 16, sl],
                                      lam, out_scale, 1))
        return jnp.concatenate(heads, axis=1)

    q_tiles = [(j * ATT_TILE, ATT_TILE, ATT_TILE) for j in range(n_full)]
    if rem:
        q_tiles.append((rows - ATT_TAIL, ATT_TAIL, rem))
    share = -(-len(q_tiles) // per)
    results = [decode(0)]
    pending = tile_scores(*q_tiles[0], 0)
    for j in range(len(q_tiles)):
        ahead = tile_scores(*q_tiles[j + 1], (j + 1) % 2) if j + 1 < len(q_tiles) else None
        tile_finish(*q_tiles[j], j % 2, *pending)
        pending = ahead
        if (j + 1) % share == 0 and len(results) < per:
            results.append(decode(len(results)))
    while len(results) < per:
        results.append(decode(len(results)))

    row8 = lax.broadcasted_iota(jnp.int32, (8, WIDTH), 0)
    merged = []
    for pair in range(ROWS_PER_STEP // 8):
        acc = results[pair * (8 // steps)]
        for j in range(1, 8 // steps):
            acc = jnp.where(row8 // steps == j, results[pair * (8 // steps) + j], acc)
        merged.append(acc)
    os_ref[...] = jnp.concatenate(merged, axis=0).astype(os_ref.dtype)


def _attention(page_table, scalars, qb, kb, vb, cache_k, cache_v, layer, batch, seq_rows,
               dec_batch, steps):
    assert seq_rows >= ATT_TILE and seq_rows % ATT_TILE <= ATT_TAIL
    assert 8 % steps == 0 and dec_batch * steps == batch * N_HEADS * ROWS_PER_STEP
    n_full = seq_rows // ATT_TILE
    kv_rows = n_full * ATT_TILE + ATT_TAIL
    n_pages = page_table.shape[1]
    sample0 = batch * seq_rows // ROWS_PER_STEP
    blk = pl.BlockSpec((seq_rows, HEAD_DIM), lambda b, h, pt: (b, h))
    sblk = pl.BlockSpec((ROWS_PER_STEP, WIDTH), lambda b, h, pt: (sample0 + b * N_HEADS + h, 0))
    grid_spec = pltpu.PrefetchScalarGridSpec(
        num_scalar_prefetch=1,
        grid=(batch, N_HEADS),
        in_specs=[pl.BlockSpec(memory_space=pltpu.SMEM), blk, blk, blk, sblk, sblk, sblk,
                  pl.BlockSpec(memory_space=pl.ANY), pl.BlockSpec(memory_space=pl.ANY)],
        out_specs=[blk, pl.BlockSpec((ROWS_PER_STEP, WIDTH), lambda b, h, pt: (b * N_HEADS + h, 0))],
        scratch_shapes=[pltpu.VMEM((kv_rows, HEAD_DIM), MXU_DTYPE),
                        pltpu.VMEM((HEAD_DIM + ONES_ROWS, (n_full + 1) * ATT_TILE), MXU_DTYPE),
                        pltpu.VMEM((2, kv_rows, 2 * ATT_TILE), F32),
                        pltpu.VMEM((2, kv_rows, 2 * ATT_TILE), MXU_DTYPE),
                        pltpu.VMEM((2, n_pages * PAGE * N_HEADS, HEAD_DIM), F32),
                        pltpu.VMEM((2, n_pages * PAGE * N_HEADS, HEAD_DIM), F32),
                        pltpu.SemaphoreType.DMA((2, 2))],
    )
    return pl.pallas_call(
        functools.partial(_attn_kernel, layer=layer, n_pages=n_pages, dec_batch=dec_batch,
                          steps=steps),
        grid_spec=grid_spec,
        out_shape=[jax.ShapeDtypeStruct((batch * seq_rows, WIDTH), MXU_DTYPE),
                   jax.ShapeDtypeStruct((dec_batch * steps, WIDTH), MXU_DTYPE)],
        compiler_params=_params("arbitrary", "arbitrary"),
        name="attention",
    )(page_table.reshape(-1), scalars, qb, kb, vb, qb, kb, vb, cache_k, cache_v)


def _collect_kernel(*refs, depth):
    k_srcs, v_srcs = refs[:depth], refs[depth:2 * depth]
    k_dst, v_dst = refs[2 * depth:]
    layer = pl.program_id(0)
    for l in range(depth):
        @pl.when(layer == l)
        def _():
            k_dst[...] = k_srcs[l][...]
            v_dst[...] = v_srcs[l][...]


def _collect(k_outs, v_outs, rows, block_rows):
    depth = len(k_outs)
    assert rows % block_rows == 0 and block_rows % 8 == 0
    n_blocks = rows // block_rows

    def src_spec(l):
        return pl.BlockSpec(
            (block_rows, HEAD_DIM),
            lambda d, i: (jnp.where(d == l, i, jnp.where(d < l, 0, n_blocks - 1)), 0))

    dst_spec = pl.BlockSpec((None, block_rows, HEAD_DIM), lambda d, i: (d, i, 0))
    stacked = jax.ShapeDtypeStruct((depth, rows, HEAD_DIM), F32)
    return pl.pallas_call(
        functools.partial(_collect_kernel, depth=depth),
        grid=(depth, n_blocks),
        in_specs=[src_spec(l) for l in range(depth)] * 2,
        out_specs=[dst_spec, dst_spec],
        out_shape=[stacked, stacked],
        compiler_params=_params("arbitrary", "arbitrary"),
        name="collect_kv",
    )(*k_outs, *v_outs)


def kernel(x_prompt, x_sample, cache_k, cache_v, state_hgrn, page_table, meta_tokens, ffn1_pre_norm, ffn1_post_norm, ffn1_w_gate, ffn1_w_up, ffn1_w_down, mix_pre_norm, mix_post_norm, w_in, hgrn_lb_logits, hgrn_norm_w, lambda_q1, lambda_k1, lambda_q2, lambda_k2, w_branch_a, w_branch_b, w_out, ffn2_pre_norm, ffn2_post_norm, ffn2_w_gate, ffn2_w_up, ffn2_w_down):
    batch, seq, d = x_prompt.shape
    dec_batch, steps = x_sample.shape[:2]
    depth = w_in.shape[0]
    seq_rows = N_META + seq
    n_prompt = batch * seq_rows
    n_sample = dec_batch * steps
    n = n_prompt + n_sample
    assert n_prompt % ROWS_PER_STEP == 0 and ROWS_PER_STEP % steps == 0
    assert (seq_rows - N_META) % (GLA_CHUNK * GLA_UNROLL) == 0
    tm = _row_tile(n, ROW_TILE_TARGET)

    lbs = jnp.cumsum(jax.nn.softmax(hgrn_lb_logits.astype(F32), axis=0), axis=0)
    lbs = lbs - lbs[0:1]
    lam_init = jnp.asarray([0.8 - 0.6 * math.exp(-0.3 * l) for l in range(depth)], F32)
    lam = (jnp.exp(jnp.sum(lambda_q1.astype(F32) * lambda_k1.astype(F32), axis=-1))
           - jnp.exp(jnp.sum(lambda_q2.astype(F32) * lambda_k2.astype(F32), axis=-1)) + lam_init)
    attn_scalars = jnp.stack([lam, 1.0 - lam_init], axis=1)

    meta = meta_tokens.astype(x_prompt.dtype)
    h = jnp.concatenate([piece for b in range(batch) for piece in (meta, x_prompt[b])]
                        + [x_sample.reshape(n_sample, d)], axis=0)
    cache_k = cache_k.reshape(cache_k.shape[:2] + (PAGE * N_HEADS, HEAD_DIM))
    cache_v = cache_v.reshape(cache_v.shape[:2] + (PAGE * N_HEADS, HEAD_DIM))
    n_front = 7 * WIDTH

    def cast(w):
        return w.astype(MXU_DTYPE)

    def vec(w):
        return w.reshape(1, -1).astype(F32)

    k_outs, v_outs, st_p, st_s = [], [], [], []
    for l in range(depth):
        h = _ffn_half(h, vec(ffn1_pre_norm[l]), vec(ffn1_post_norm[l]), cast(ffn1_w_gate[l]),
                      cast(ffn1_w_up[l]), cast(ffn1_w_down[l]), tm)
        qa, ka, va, lf, zg, qb, kb, vb, k_out, v_out = _mixer_front(
            h, vec(mix_pre_norm[l]), cast(w_in[l, :, :n_front]), vec(lbs[l]), tm)
        oa_p, s_p = _gla_prompt(qa, ka, va, lf, batch, seq_rows)
        oa_s, s_s = _gla_sample(qa, ka, va, lf, state_hgrn, l, n_prompt,
                                dec_batch, steps)
        ob_p, ob_s = _attention(page_table, attn_scalars[l], qb, kb, vb, cache_k, cache_v, l,
                                batch, seq_rows, dec_batch, steps)
        h = _mixer_back(h, jnp.concatenate([oa_p, oa_s], axis=0), zg,
                        jnp.concatenate([ob_p, ob_s], axis=0), vec(mix_pre_norm[l]),
                        vec(mix_post_norm[l]), vec(hgrn_norm_w[l]), cast(w_in[l, :, n_front:]),
                        cast(w_branch_a[l]), cast(w_branch_b[l]), cast(w_out[l]), tm)
        h = _ffn_half(h, vec(ffn2_pre_norm[l]), vec(ffn2_post_norm[l]), cast(ffn2_w_gate[l]),
                      cast(ffn2_w_up[l]), cast(ffn2_w_down[l]), tm)
        k_outs.append(k_out)
        v_outs.append(v_out)
        st_p.append(s_p.astype(state_hgrn.dtype))
        st_s.append(s_s.astype(state_hgrn.dtype))
    split = n_prompt * N_HEADS
    k_p, v_p = _collect(k_outs, v_outs, split, seq_rows)
    k_s = jnp.stack([k[split:] for k in k_outs])
    v_s = jnp.stack([v[split:] for v in v_outs])
    prompt_shape = (depth, batch, seq_rows, N_HEADS, HEAD_DIM)
    sample_shape = (depth, dec_batch, steps, N_HEADS, HEAD_DIM)
    y_prompt = h[:n_prompt].reshape(batch, seq_rows, d)[:, N_META:]
    y_sample = h[n_prompt:].reshape(dec_batch, steps, d)
    return (y_prompt, y_sample, k_p.reshape(prompt_shape), v_p.reshape(prompt_shape),
            jnp.stack(st_p), k_s.reshape(sample_shape), v_s.reshape(sample_shape),
            jnp.stack(st_s))
```
